```python
import jax, jax.numpy as jnp
from jax import lax
import numpy as np

D_MODEL = 4096
BATCH = 4
SEQ = 4096
DEPTH = 2

N_META = 16
NORM_EPS = 1e-5
POOL_WINDOWS = (2, 4, 8, 16)
POOL_WIDTH = D_MODEL // 2
POOL_GROUP = POOL_WIDTH // len(POOL_WINDOWS)
HGRN_HEAD_DIM = 128
HGRN_WIDTH = D_MODEL // 2
HGRN_HEADS = HGRN_WIDTH // HGRN_HEAD_DIM
HGRN_CHUNK = 64
EVEN_IN_WIDTH = POOL_WIDTH + 4 * HGRN_WIDTH
EVEN_OUT_WIDTH = POOL_WIDTH + HGRN_WIDTH
ATT_HEAD_DIM = 64
ATT_Q_HEADS = D_MODEL // ATT_HEAD_DIM
ATT_KV_HEADS = 8
ATT_GROUP = ATT_Q_HEADS // ATT_KV_HEADS
ATT_WINDOW = 128
ATT_BLOCK = 128
ODD_IN_WIDTH = (ATT_Q_HEADS + 2 * ATT_KV_HEADS) * ATT_HEAD_DIM
ODD_OUT_WIDTH = ATT_Q_HEADS * ATT_HEAD_DIM
DENSE_FF = 11008
N_EXPERTS = 8
TOP_K = 2
EXPERT_FF = 6144
MOE_BLOCK = 256
N_EVEN = (DEPTH + 1) // 2
N_ODD = DEPTH // 2

kernel_name = "hybrid_pool_hgrn2_swa_moe_trunk"

F32 = jnp.float32


def rms_norm(x, gain):
    xf = x.astype(F32)
    y = xf * lax.rsqrt(jnp.mean(xf * xf, axis=-1, keepdims=True) + NORM_EPS)
    return (y * gain.astype(F32)).astype(x.dtype)


def multiscale_pool(u, w_group, scale):
    b, t, _ = u.shape
    ug = u.astype(F32).reshape(b, t, len(POOL_WINDOWS), POOL_GROUP)
    cs = jnp.cumsum(ug, axis=1)
    pos = jnp.arange(t)
    outs = []
    for gi, w in enumerate(POOL_WINDOWS):
        c = cs[:, :, gi]
        lagged = jnp.pad(c, ((0, 0), (w, 0), (0, 0)))[:, :t]
        count = jnp.minimum(pos + 1, w).astype(F32)[None, :, None]
        outs.append((c - lagged) / count - ug[:, :, gi])
    d = jnp.stack(outs, axis=2)
    y = jnp.einsum('btgc,gcd->btgd', d, w_group.astype(F32))
    return (y.reshape(b, t, POOL_WIDTH) * scale.astype(F32)).astype(u.dtype)


def hgrn2(q_raw, f_raw, i_raw, g_raw, lower_bound, norm_gain):
    b, t, _ = q_raw.shape
    pad = HGRN_CHUNK - N_META
    fg = lower_bound.astype(F32) + (1.0 - lower_bound.astype(F32)) * jax.nn.sigmoid(f_raw.astype(F32))
    log_f = jnp.log(fg)
    k = 1.0 - fg
    q = jax.nn.silu(q_raw.astype(F32))
    v = i_raw.astype(F32)
    nc = (t + pad) // HGRN_CHUNK

    def to_chunks(a):
        a = jnp.pad(a, ((0, 0), (pad, 0), (0, 0)))
        a = a.reshape(b, nc, HGRN_CHUNK, HGRN_HEADS, HGRN_HEAD_DIM)
        return a.transpose(1, 0, 3, 2, 4)

    qc, kc, vc, gc = to_chunks(q), to_chunks(k), to_chunks(v), to_chunks(log_f)
    causal = jnp.tril(jnp.ones((HGRN_CHUNK, HGRN_CHUNK), bool))[:, :, None]

    def step(state, inp):
        qb, kb, vb, gb = inp
        cum = jnp.cumsum(gb, axis=2)
        o_inter = jnp.einsum('bhck,bhkv->bhcv', qb * jnp.exp(cum), state)
        rel = cum[:, :, :, None, :] - cum[:, :, None, :, :]
        decay = jnp.exp(jnp.where(causal, rel, -jnp.inf))
        scores = jnp.einsum('bhtk,bhsk,bhtsk->bhts', qb, kb, decay)
        o_intra = jnp.einsum('bhts,bhsv->bhtv', scores, vb)
        last = cum[:, :, -1:, :]
        new_state = state * jnp.exp(last[:, :, 0, :, None]) + jnp.einsum(
            'bhsk,bhsv->bhkv', kb * jnp.exp(last - cum), vb)
        return new_state, o_inter + o_intra

    s0 = jnp.zeros((b, HGRN_HEADS, HGRN_HEAD_DIM, HGRN_HEAD_DIM), F32)
    _, o = lax.scan(step, s0, (qc, kc, vc, gc))
    o = o.transpose(1, 0, 3, 2, 4).reshape(b, nc * HGRN_CHUNK, HGRN_HEADS, HGRN_HEAD_DIM)[:, pad:]
    o = o * lax.rsqrt(jnp.mean(o * o, axis=-1, keepdims=True) + NORM_EPS)
    o = o.reshape(b, t, HGRN_WIDTH) * norm_gain.astype(F32) * jax.nn.silu(g_raw.astype(F32))
    return o.astype(q_raw.dtype)


def alibi_slopes(n_heads):
    return 2.0 ** (-8.0 * jnp.arange(1, n_heads + 1, dtype=F32) / n_heads)


def sliding_window_attention(q, k, v, sinks):
    b, t = q.shape[:2]
    out_dtype = q.dtype
    pad = ATT_BLOCK - N_META
    padt = lambda a: jnp.pad(a.astype(F32), ((0, 0), (pad, 0), (0, 0), (0, 0)))
    q, k, v = padt(q), padt(k), padt(v)
    lp = t + pad
    nb = lp // ATT_BLOCK
    qb = q.reshape(b, nb, ATT_BLOCK, ATT_KV_HEADS, ATT_GROUP, ATT_HEAD_DIM).transpose(1, 0, 2, 3, 4, 5)

    def band(a):
        a = a.reshape(b, nb, ATT_BLOCK, ATT_KV_HEADS, ATT_HEAD_DIM)
        prev = jnp.pad(a, ((0, 0), (1, 0), (0, 0), (0, 0), (0, 0)))[:, :nb]
        return jnp.concatenate([prev, a], axis=2).transpose(1, 0, 2, 3, 4)

    k_band, v_band = band(k), band(v)
    k_meta, v_meta = k[:, pad:pad + N_META], v[:, pad:pad + N_META]
    slopes = alibi_slopes(ATT_Q_HEADS).reshape(ATT_KV_HEADS, ATT_GROUP)[:, :, None, None]
    sink = sinks.astype(F32).reshape(ATT_KV_HEADS, ATT_GROUP)[None, :, :, None, None]
    scale = ATT_HEAD_DIM ** -0.5
    q_off = jnp.arange(ATT_BLOCK)
    s_off = jnp.arange(2 * ATT_BLOCK) - ATT_BLOCK
    meta_pos = pad + jnp.arange(N_META)

    def block(args):
        i, qi, ki, vi = args
        t_pos = i * ATT_BLOCK + q_off
        s_pos = i * ATT_BLOCK + s_off
        dist = t_pos[:, None] - s_pos[None, :]
        band_ok = (dist >= 0) & (dist < ATT_WINDOW) & (s_pos[None, :] >= ATT_BLOCK)
        meta_ok = meta_pos[None, :] <= t_pos[:, None]
        lb = jnp.einsum('bqkgd,bskd->bkgqs', qi, ki) * scale - slopes * dist.astype(F32)
        lb = jnp.where(band_ok, lb, -jnp.inf)
        lm = jnp.einsum('bqkgd,bmkd->bkgqm', qi, k_meta) * scale
        lm = jnp.where(meta_ok, lm, -jnp.inf)
        mx = jnp.maximum(jnp.maximum(lb.max(-1, keepdims=True), lm.max(-1, keepdims=True)), sink)
        pb = jnp.exp(lb - mx)
        pm = jnp.exp(lm - mx)
        denom = pb.sum(-1) + pm.sum(-1) + jnp.exp(sink - mx)[..., 0]
        o = jnp.einsum('bkgqs,bskd->bqkgd', pb, vi) + jnp.einsum('bkgqm,bmkd->bqkgd', pm, v_meta)
        return o / denom.transpose(0, 3, 1, 2)[..., None]

    o = lax.map(block, (jnp.arange(nb), qb, k_band, v_band))
    o = o.transpose(1, 0, 2, 3, 4, 5).reshape(b, lp, ODD_OUT_WIDTH)[:, pad:]
    return o.astype(out_dtype)


def dense_swiglu(u, w_gate, w_up, w_down):
    return (jax.nn.silu(u @ w_gate) * (u @ w_up)) @ w_down


def moe_swiglu(u, router, w_gate, w_up, w_down):
    b, t, d = u.shape
    n = b * t
    uf = u.reshape(n, d)
    logits = (uf @ router).astype(F32)
    top_val, top_idx = lax.top_k(logits, TOP_K)
    gates = jax.nn.softmax(top_val, axis=-1)
    n_assign = n * TOP_K
    e_flat = top_idx.reshape(-1)
    tok_flat = jnp.arange(n_assign) // TOP_K
    g_flat = gates.reshape(-1)
    order = jnp.argsort(e_flat)
    e_sorted = e_flat[order]
    counts = jnp.bincount(e_flat, length=N_EXPERTS)
    starts = jnp.cumsum(counts) - counts
    padded = (counts + MOE_BLOCK - 1) // MOE_BLOCK * MOE_BLOCK
    pad_end = jnp.cumsum(padded)
    pad_start = pad_end - padded
    n_blocks = -(-n_assign // MOE_BLOCK) + N_EXPERTS
    slot = pad_start[e_sorted] + jnp.arange(n_assign) - starts[e_sorted]
    slot_tok = jnp.full((n_blocks * MOE_BLOCK,), n, jnp.int32).at[slot].set(tok_flat[order].astype(jnp.int32))
    slot_gate = jnp.zeros((n_blocks * MOE_BLOCK,), F32).at[slot].set(g_flat[order])
    block_expert = jnp.minimum(
        jnp.searchsorted(pad_end, jnp.arange(n_blocks) * MOE_BLOCK, side='right'), N_EXPERTS - 1)
    u_pad = jnp.concatenate([uf, jnp.zeros((1, d), uf.dtype)], axis=0)

    def expert_block(args):
        e, toks, g = args
        xb = u_pad[toks]
        h = jax.nn.silu(xb @ w_gate[e]) * (xb @ w_up[e])
        return (h @ w_down[e]) * g[:, None].astype(xb.dtype)

    y = lax.map(expert_block, (block_expert, slot_tok.reshape(n_blocks, MOE_BLOCK),
                               slot_gate.reshape(n_blocks, MOE_BLOCK)))
    out = jnp.zeros((n + 1, d), F32).at[slot_tok].add(y.reshape(-1, d).astype(F32))[:n]
    return out.reshape(b, t, d).astype(u.dtype)


def setup_inputs(seed: int = 0) -> dict:
    key = jax.random.key(seed)
    ks = jax.random.split(key, 24)
    nrm = lambda k, shape, fan_in: jax.random.normal(k, shape, F32) * (fan_in ** -0.5)
    gain = lambda k, shape: 1.0 + 0.02 * jax.random.normal(k, shape, F32)
    return {
        "x": jax.random.normal(ks[0], (BATCH, SEQ, D_MODEL), F32),
        "meta_tokens": jax.random.normal(ks[1], (N_META, D_MODEL), F32),
        "norm_mix": gain(ks[2], (DEPTH, D_MODEL)),
        "norm_ffn": gain(ks[3], (DEPTH, D_MODEL)),
        "final_norm": gain(ks[4], (D_MODEL,)),
        "even_w_in": nrm(ks[5], (N_EVEN, D_MODEL, EVEN_IN_WIDTH), D_MODEL),
        "even_w_out": nrm(ks[6], (N_EVEN, EVEN_OUT_WIDTH, D_MODEL), EVEN_OUT_WIDTH),
        "pool_w_group": nrm(ks[7], (N_EVEN, len(POOL_WINDOWS), POOL_GROUP, POOL_GROUP), POOL_GROUP),
        "pool_scale": gain(ks[8], (N_EVEN, POOL_WIDTH)),
        "hgrn_lb_logits": 1.0 + 0.5 * jax.random.normal(ks[9], (N_EVEN + 1, HGRN_WIDTH), F32),
        "hgrn_norm": gain(ks[10], (N_EVEN, HGRN_WIDTH)),
        "odd_w_in": nrm(ks[11], (N_ODD, D_MODEL, ODD_IN_WIDTH), D_MODEL),
        "odd_w_out": nrm(ks[12], (N_ODD, ODD_OUT_WIDTH, D_MODEL), ODD_OUT_WIDTH),
        "attn_sinks": 0.5 * jax.random.normal(ks[13], (N_ODD, ATT_Q_HEADS), F32),
        "ffn_w_gate": nrm(ks[14], (N_EVEN, D_MODEL, DENSE_FF), D_MODEL),
        "ffn_w_up": nrm(ks[15], (N_EVEN, D_MODEL, DENSE_FF), D_MODEL),
        "ffn_w_down": nrm(ks[16], (N_EVEN, DENSE_FF, D_MODEL), DENSE_FF),
        "moe_router": nrm(ks[17], (N_ODD, D_MODEL, N_EXPERTS), D_MODEL),
        "moe_w_gate": nrm(ks[18], (N_ODD, N_EXPERTS, D_MODEL, EXPERT_FF), D_MODEL),
        "moe_w_up": nrm(ks[19], (N_ODD, N_EXPERTS, D_MODEL, EXPERT_FF), D_MODEL),
        "moe_w_down": nrm(ks[20], (N_ODD, N_EXPERTS, EXPERT_FF, D_MODEL), EXPERT_FF),
    }


def reference(x, meta_tokens, norm_mix, norm_ffn, final_norm, even_w_in, even_w_out,
              pool_w_group, pool_scale, hgrn_lb_logits, hgrn_norm, odd_w_in, odd_w_out,
              attn_sinks, ffn_w_gate, ffn_w_up, ffn_w_down, moe_router, moe_w_gate,
              moe_w_up, moe_w_down):
    b = x.shape[0]
    meta = jnp.broadcast_to(meta_tokens.astype(x.dtype)[None], (b, N_META, D_MODEL))
    h = jnp.concatenate([meta, x], axis=1)
    t = h.shape[1]
    lower_bounds = jnp.cumsum(jax.nn.softmax(hgrn_lb_logits.astype(F32), axis=0), axis=0)
    split_at = [POOL_WIDTH + r * HGRN_WIDTH for r in range(4)]
    nq = ATT_Q_HEADS * ATT_HEAD_DIM
    nkv = ATT_KV_HEADS * ATT_HEAD_DIM
    for layer in range(DEPTH):
        j = layer // 2
        u = rms_norm(h, norm_mix[layer])
        if layer % 2 == 0:
            z = u @ even_w_in[j]
            z_pool, z_q, z_f, z_i, z_g = jnp.split(z, split_at, axis=-1)
            y_a = multiscale_pool(z_pool, pool_w_group[j], pool_scale[j])
            y_b = hgrn2(z_q, z_f, z_i, z_g, lower_bounds[j], hgrn_norm[j])
            h = h + jnp.concatenate([y_a, y_b], axis=-1) @ even_w_out[j]
            h = h + dense_swiglu(rms_norm(h, norm_ffn[layer]), ffn_w_gate[j], ffn_w_up[j], ffn_w_down[j])
        else:
            z = u @ odd_w_in[j]
            q = z[..., :nq].reshape(b, t, ATT_Q_HEADS, ATT_HEAD_DIM)
            k = z[..., nq:nq + nkv].reshape(b, t, ATT_KV_HEADS, ATT_HEAD_DIM)
            v = z[..., nq + nkv:].reshape(b, t, ATT_KV_HEADS, ATT_HEAD_DIM)
            h = h + sliding_window_attention(q, k, v, attn_sinks[j]) @ odd_w_out[j]
            h = h + moe_swiglu(rms_norm(h, norm_ffn[layer]), moe_router[j], moe_w_gate[j],
                               moe_w_up[j], moe_w_down[j])
    h = rms_norm(h, final_norm)
    return h[:, N_META:]
```

```python
import functools
import math

import numpy as np
import jax
import jax.numpy as jnp
from jax import lax
from jax.experimental import pallas as pl
from jax.experimental.pallas import tpu as pltpu

F32 = jnp.float32
BF16 = jnp.bfloat16

N_META = 16
NORM_EPS = 1e-5
POOL_WINDOWS = (2, 4, 8, 16)
POOL_HALO = 16
HGRN_HEAD_DIM = 128
HGRN_CHUNK = 128
HGRN_HEADS_PER_STEP = 4
ATT_HEAD_DIM = 64
ATT_KV_HEADS = 8
ATT_BLOCK = 128
N_EXPERTS = 8
TOP_K = 2

META_BLOCK = 128
ROW_TILE = 512
LANES = 128
VMEM_LIMIT = 56 * 1024 * 1024


def _params(*sem):
    return pltpu.CompilerParams(dimension_semantics=sem, vmem_limit_bytes=VMEM_LIMIT)


def _silu(x):
    return x * jax.nn.sigmoid(x)


def _dot(a, b):
    return jnp.dot(a, b, preferred_element_type=F32)


def _dot_nt(a, b):
    return lax.dot_general(a, b, (((1,), (1,)), ((), ())), preferred_element_type=F32)


def _dot_tn(a, b):
    return lax.dot_general(a, b, (((0,), (0,)), ((), ())), preferred_element_type=F32)


def _rmsnorm_kernel(h_ref, g_ref, o_ref):
    x = h_ref[...]
    y = x * lax.rsqrt(jnp.mean(x * x, axis=-1, keepdims=True) + NORM_EPS)
    o_ref[...] = (y * g_ref[...]).astype(o_ref.dtype)


def _rmsnorm(h, gain, out_dtype=BF16, tm=ROW_TILE):
    m, d = h.shape
    return pl.pallas_call(
        _rmsnorm_kernel,
        grid=(m // tm,),
        in_specs=[pl.BlockSpec((tm, d), lambda i: (i, 0)), pl.BlockSpec((1, d), lambda i: (0, 0))],
        out_specs=pl.BlockSpec((tm, d), lambda i: (i, 0)),
        out_shape=jax.ShapeDtypeStruct((m, d), out_dtype),
        compiler_params=_params("arbitrary"),
    )(h, gain.reshape(1, d))


def _mm_kernel(a_ref, w_ref, o_ref):
    o_ref[...] = _dot(a_ref[...], w_ref[...]).astype(o_ref.dtype)


def _mm_res_kernel(a_ref, w_ref, r_ref, o_ref):
    o_ref[...] = r_ref[...] + _dot(a_ref[...], w_ref[...])


def _mm_swiglu_kernel(a_ref, wg_ref, wu_ref, o_ref):
    a = a_ref[...]
    o_ref[...] = (_silu(_dot(a, wg_ref[...])) * _dot(a, wu_ref[...])).astype(o_ref.dtype)


def _matmul(a, w, *, tn, out_dtype, n_outer, tm=ROW_TILE, m_rows=None):
    m = a.shape[0] if m_rows is None else m_rows
    k = a.shape[1]
    n = w.shape[1]
    if n_outer:
        grid = (n // tn, m // tm)
        mi, ni = (lambda j, i: (i, 0)), (lambda j, i: (0, j))
        oi = lambda j, i: (i, j)
    else:
        grid = (m // tm, n // tn)
        mi, ni = (lambda i, j: (i, 0)), (lambda i, j: (0, j))
        oi = lambda i, j: (i, j)
    return pl.pallas_call(
        _mm_kernel,
        grid=grid,
        in_specs=[pl.BlockSpec((tm, k), mi), pl.BlockSpec((k, tn), ni)],
        out_specs=pl.BlockSpec((tm, tn), oi),
        out_shape=jax.ShapeDtypeStruct((m, n), out_dtype),
        compiler_params=_params("arbitrary", "arbitrary"),
    )(a, w)


def _matmul_residual(a, w, res, *, tn, tm=ROW_TILE, m_rows=None):
    m = a.shape[0] if m_rows is None else m_rows
    k = a.shape[1]
    n = w.shape[1]
    return pl.pallas_call(
        _mm_res_kernel,
        grid=(m // tm, n // tn),
        in_specs=[pl.BlockSpec((tm, k), lambda i, j: (i, 0)),
                  pl.BlockSpec((k, tn), lambda i, j: (0, j)),
                  pl.BlockSpec((tm, tn), lambda i, j: (i, j))],
        out_specs=pl.BlockSpec((tm, tn), lambda i, j: (i, j)),
        out_shape=jax.ShapeDtypeStruct((m, n), F32),
        compiler_params=_params("arbitrary", "arbitrary"),
    )(a, w, res)


def _matmul_swiglu(a, wg, wu, *, tn, tm=ROW_TILE):
    m, k = a.shape
    n = wg.shape[1]
    return pl.pallas_call(
        _mm_swiglu_kernel,
        grid=(n // tn, m // tm),
        in_specs=[pl.BlockSpec((tm, k), lambda j, i: (i, 0)),
                  pl.BlockSpec((k, tn), lambda j, i: (0, j)),
                  pl.BlockSpec((k, tn), lambda j, i: (0, j))],
        out_specs=pl.BlockSpec((tm, tn), lambda j, i: (i, j)),
        out_shape=jax.ShapeDtypeStruct((m, n), BF16),
        compiler_params=_params("arbitrary", "arbitrary"),
    )(a, wg, wu)


def _pool_kernel(z_ref, halo_ref, wg_ref, sc_ref, o_ref, *, tm, n_real, group):
    i = pl.program_id(0)
    rows = i * tm + lax.broadcasted_iota(jnp.int32, (tm, 1), 0)
    in_meta = rows >= n_real
    pos = (rows % META_BLOCK) - (META_BLOCK - N_META)
    is_pad = in_meta & (pos < 0)
    for gi, w in enumerate(POOL_WINDOWS):
        cols = slice(gi * group, (gi + 1) * group)
        x = z_ref[:, cols]
        s = jnp.concatenate([halo_ref[:, cols], x], axis=0)
        step = 1
        while step < w:
            s = s + pltpu.roll(s, step, axis=0)
            step *= 2
        count = jnp.where(in_meta, jnp.clip(pos + 1, 1, w), w).astype(F32)
        d = s[POOL_HALO:] / count - x
        y = _dot(d.astype(BF16), wg_ref[gi]) * sc_ref[:, cols]
        o_ref[:, cols] = jnp.where(is_pad, 0.0, y).astype(o_ref.dtype)


def _pool_mixer(z, w_group, scale, *, n_real, seq, tm=ROW_TILE):
    m = z.shape[0]
    n_g, group, _ = w_group.shape
    width = n_g * group
    tiles_per_seq = seq // tm
    hb = tm // POOL_HALO

    def halo_index(i):
        b = i // tiles_per_seq
        meta_tail = (n_real + b * META_BLOCK + META_BLOCK - POOL_HALO) // POOL_HALO
        real = jnp.where(i % tiles_per_seq == 0, meta_tail, i * hb - 1)
        return (jnp.where(i * tm >= n_real, n_real // POOL_HALO, real), 0)

    return pl.pallas_call(
        functools.partial(_pool_kernel, tm=tm, n_real=n_real, group=group),
        grid=(m // tm,),
        in_specs=[pl.BlockSpec((tm, width), lambda i: (i, 0)),
                  pl.BlockSpec((POOL_HALO, width), halo_index),
                  pl.BlockSpec((n_g, group, group), lambda i: (0, 0, 0)),
                  pl.BlockSpec((1, width), lambda i: (0, 0))],
        out_specs=pl.BlockSpec((tm, width), lambda i: (i, 0)),
        out_shape=jax.ShapeDtypeStruct((m, width), BF16),
        compiler_params=_params("arbitrary"),
    )(z, z, w_group, scale.reshape(1, width))


def _hgrn_tables(c):
    levels = int(math.log2(c))
    t = np.arange(c)
    blocks = [(t[None, :] <= t[:, None]), (t[None, :] > t[:, None])]
    for l in range(1, levels + 1):
        bs, hs = 1 << l, 1 << (l - 1)
        mid = (t // bs) * bs + hs - 1
        upper = (t % bs) >= hs
        eq = upper[:, None] & (t[None, :] > mid[:, None]) & (t[None, :] <= t[:, None])
        ek = (~upper)[:, None] & (t[None, :] > t[:, None]) & (t[None, :] <= mid[:, None])
        blocks += [eq, ek]
    sums = np.concatenate(blocks, axis=0).astype(np.float32)
    x = t[:, None] ^ t[None, :]
    lev = np.where(t[None, :] > t[:, None], -1,
                   np.floor(np.log2(np.maximum(x, 1))).astype(np.int32) + (x > 0))
    return sums, lev.astype(np.int32), levels


def _hgrn_kernel(q_ref, f_ref, i_ref, g_ref, lbl_ref, gain_ref, sums_ref, lev_ref, o_ref, state_ref,
                 *, c, levels, heads, layer):
    @pl.when(pl.program_id(2) == 0)
    def _():
        state_ref[...] = jnp.zeros_like(state_ref)

    sums = sums_ref[...]
    lev = lev_ref[...]
    hd = HGRN_HEAD_DIM
    lbl = lbl_ref[...]
    ex = jnp.exp(lbl - lbl.max(0, keepdims=True))
    lb_all = ex[0:layer + 1].sum(0, keepdims=True) / ex.sum(0, keepdims=True)
    for h in range(heads):
        cols = slice(h * hd, (h + 1) * hd)
        lb = lb_all[:, cols]
        fg = lb + (1.0 - lb) * jax.nn.sigmoid(f_ref[:, cols])
        log_f = jnp.log(fg)
        k = 1.0 - fg
        q = _silu(q_ref[:, cols])
        v = i_ref[:, cols].astype(BF16)
        hi = log_f.astype(BF16)
        r1 = log_f - hi.astype(F32)
        mid = r1.astype(BF16)
        lo = (r1 - mid.astype(F32)).astype(BF16)
        e = _dot(sums, hi) + _dot(sums, mid) + _dot(sums, lo)
        cum = e[0:c]
        to_end = e[c:2 * c]
        st = state_ref[h]
        o = _dot_nt((q * jnp.exp(cum)).astype(BF16), st.astype(BF16))
        kd = (k * jnp.exp(to_end)).astype(BF16)
        state_ref[h] = st * jnp.exp(cum[c - 1:c, :]) + _dot_tn(v, kd)
        scores = jnp.where(lev == 0, _dot_nt(q.astype(BF16), k.astype(BF16)), 0.0)
        for l in range(1, levels + 1):
            eq = e[2 * l * c:(2 * l + 1) * c]
            ek = e[(2 * l + 1) * c:(2 * l + 2) * c]
            s_l = _dot_nt((q * jnp.exp(eq)).astype(BF16), (k * jnp.exp(ek)).astype(BF16))
            scores = scores + jnp.where(lev == l, s_l, 0.0)
        o = o + _dot(scores.astype(BF16), v)
        o = o * lax.rsqrt(jnp.mean(o * o, axis=-1, keepdims=True) + NORM_EPS)
        o_ref[:, cols] = (o * gain_ref[:, cols] * _silu(g_ref[:, cols])).astype(o_ref.dtype)


def _hgrn_mixer(z, col0, lb_logits, layer, norm_gain, *, batch, n_real, seq):
    m = z.shape[0]
    n_lb, width = lb_logits.shape
    c = HGRN_CHUNK
    assert META_BLOCK == c
    hps = HGRN_HEADS_PER_STEP
    wb = hps * HGRN_HEAD_DIM
    n_hg = width // wb
    chunks = seq // c + 1
    sums, lev, levels = _hgrn_tables(c)

    def row_block(b, ci):
        return jnp.where(ci == 0, n_real // c + b, b * (seq // c) + ci - 1)

    def zspec(r):
        cb = (col0 + r * width) // wb
        return pl.BlockSpec((c, wb), lambda b, hg, ci: (row_block(b, ci), cb + hg))

    vec = pl.BlockSpec((1, wb), lambda b, hg, ci: (0, hg))
    return pl.pallas_call(
        functools.partial(_hgrn_kernel, c=c, levels=levels, heads=hps, layer=layer),
        grid=(batch, n_hg, chunks),
        in_specs=[zspec(0), zspec(1), zspec(2), zspec(3),
                  pl.BlockSpec((n_lb, wb), lambda b, hg, ci: (0, hg)), vec,
                  pl.BlockSpec(sums.shape, lambda b, hg, ci: (0, 0)),
                  pl.BlockSpec(lev.shape, lambda b, hg, ci: (0, 0))],
        out_specs=pl.BlockSpec((c, wb), lambda b, hg, ci: (row_block(b, ci), hg)),
        out_shape=jax.ShapeDtypeStruct((m, width), BF16),
        scratch_shapes=[pltpu.VMEM((hps, HGRN_HEAD_DIM, HGRN_HEAD_DIM), F32)],
        compiler_params=_params("arbitrary", "arbitrary", "arbitrary"),
    )(z, z, z, z, lb_logits.astype(F32), norm_gain.reshape(1, width),
      jnp.asarray(sums, BF16), jnp.asarray(lev))


def _attn_kernel(slope_ref, sink_ref, q_ref, kvc_ref, kvp_ref, kvm_ref, o_ref, *, group):
    i = pl.program_id(1)
    j = pl.program_id(2)
    hd = ATT_HEAD_DIM
    blk = ATT_BLOCK
    kvc, kvp, kvm = kvc_ref[...], kvp_ref[...], kvm_ref[...]
    k_band = jnp.concatenate([kvp[:, :hd], kvc[:, :hd]], axis=0)
    v_band = jnp.concatenate([kvp[:, hd:], kvc[:, hd:]], axis=0)
    k_meta, v_meta = kvm[:, :hd], kvm[:, hd:]
    r = lax.broadcasted_iota(jnp.int32, (blk, 2 * blk), 0)
    cidx = lax.broadcasted_iota(jnp.int32, (blk, 2 * blk), 1)
    dist = r - cidx + blk
    band_ok = (dist >= 0) & (dist < blk) & ((cidx >= blk) | (i > 0))
    dist_f = dist.astype(F32)
    meta_ok = lax.broadcasted_iota(jnp.int32, (blk, META_BLOCK), 1) >= META_BLOCK - N_META
    scale = hd ** -0.5
    for g in range(group):
        head = j * group + g
        slope, sink = slope_ref[head], sink_ref[head]
        qh = (q_ref[:, g * hd:(g + 1) * hd].astype(F32) * scale).astype(BF16)
        lb = jnp.where(band_ok, _dot_nt(qh, k_band) - slope * dist_f, -jnp.inf)
        lm = jnp.where(meta_ok, _dot_nt(qh, k_meta), -jnp.inf)
        mx = jnp.maximum(jnp.maximum(lb.max(-1, keepdims=True), lm.max(-1, keepdims=True)), sink)
        pb = jnp.exp(lb - mx)
        pm = jnp.exp(lm - mx)
        denom = pb.sum(-1, keepdims=True) + pm.sum(-1, keepdims=True) + jnp.exp(sink - mx)
        o = _dot(pb.astype(BF16), v_band) + _dot(pm.astype(BF16), v_meta)
        o_ref[:, g * hd:(g + 1) * hd] = (o / denom).astype(o_ref.dtype)


def _attention(z, slopes, sinks, *, batch, n_real, seq):
    hd = ATT_HEAD_DIM
    n_q = slopes.shape[0]
    group = n_q // ATT_KV_HEADS
    qw = group * hd
    kv0 = n_q * hd // (2 * hd)
    nblk = seq // ATT_BLOCK
    smem = pl.BlockSpec(memory_space=pltpu.SMEM)
    kv_spec = lambda rows: pl.BlockSpec((ATT_BLOCK, 2 * hd), lambda b, i, j: (rows(b, i), kv0 + j))
    return pl.pallas_call(
        functools.partial(_attn_kernel, group=group),
        grid=(batch, nblk, ATT_KV_HEADS),
        in_specs=[smem, smem,
                  pl.BlockSpec((ATT_BLOCK, qw), lambda b, i, j: (b * nblk + i, j)),
                  kv_spec(lambda b, i: b * nblk + i),
                  kv_spec(lambda b, i: b * nblk + jnp.maximum(i - 1, 0)),
                  kv_spec(lambda b, i: n_real // ATT_BLOCK + b)],
        out_specs=pl.BlockSpec((ATT_BLOCK, qw), lambda b, i, j: (b * nblk + i, j)),
        out_shape=jax.ShapeDtypeStruct((n_real, n_q * hd), BF16),
        compiler_params=_params("arbitrary", "arbitrary", "arbitrary"),
    )(slopes, sinks, z, z, z, z)


def _route_kernel(h_ref, g_ref, r_ref, u_ref, route_ref, count_ref, carry_ref, *, tm):
    @pl.when(pl.program_id(0) == 0)
    def _():
        carry_ref[...] = jnp.zeros_like(carry_ref)

    x = h_ref[...]
    u = x * lax.rsqrt(jnp.mean(x * x, axis=-1, keepdims=True) + NORM_EPS) * g_ref[...]
    u_ref[...] = u
    logits = jnp.dot(u, r_ref[...], preferred_element_type=F32, precision=lax.Precision.HIGHEST)
    lane = lax.broadcasted_iota(jnp.int32, (tm, LANES), 1)
    lg = jnp.where(lane < N_EXPERTS, logits, -jnp.inf)
    m1 = lg.max(-1, keepdims=True)
    i1 = jnp.where(lg == m1, lane, LANES).min(-1, keepdims=True)
    lg2 = jnp.where(lane == i1, -jnp.inf, lg)
    m2 = lg2.max(-1, keepdims=True)
    i2 = jnp.where(lg2 == m2, lane, LANES).min(-1, keepdims=True)
    e2 = jnp.exp(m2 - m1)
    g1 = 1.0 / (1.0 + e2)
    g2 = e2 / (1.0 + e2)
    chosen = (lane == i1) | (lane == i2)
    onehot = jnp.where(chosen, 1.0, 0.0)
    rr = lax.broadcasted_iota(jnp.int32, (tm, tm), 0)
    cc = lax.broadcasted_iota(jnp.int32, (tm, tm), 1)
    before = jnp.where(cc < rr, 1.0, 0.0).astype(BF16)
    rank = _dot(before, onehot.astype(BF16)) + carry_ref[...]
    r1 = jnp.where(lane == i1, rank, 0.0).sum(-1, keepdims=True)
    r2 = jnp.where(lane == i2, rank, 0.0).sum(-1, keepdims=True)
    carry_ref[...] = carry_ref[...] + onehot.sum(0, keepdims=True)
    count_ref[...] = carry_ref[...]
    packed = jnp.where(lane == 0, i1.astype(F32), 0.0)
    for idx, val in ((1, i2.astype(F32)), (2, g1), (3, g2), (4, r1), (5, r2)):
        packed = jnp.where(lane == idx, val, packed)
    route_ref[...] = packed


def _route(h, gain, router, *, n_rows, tm=ROW_TILE):
    d = h.shape[1]
    router_pad = jnp.zeros((d, LANES), F32).at[:, :N_EXPERTS].set(router.astype(F32))
    return pl.pallas_call(
        functools.partial(_route_kernel, tm=tm),
        grid=(n_rows // tm,),
        in_specs=[pl.BlockSpec((tm, d), lambda i: (i, 0)),
                  pl.BlockSpec((1, d), lambda i: (0, 0)),
                  pl.BlockSpec((d, LANES), lambda i: (0, 0))],
        out_specs=[pl.BlockSpec((tm, d), lambda i: (i, 0)),
                   pl.BlockSpec((tm, LANES), lambda i: (i, 0)),
                   pl.BlockSpec((1, LANES), lambda i: (0, 0))],
        out_shape=[jax.ShapeDtypeStruct((n_rows, d), F32),
                   jax.ShapeDtypeStruct((n_rows, LANES), F32),
                   jax.ShapeDtypeStruct((1, LANES), F32)],
        scratch_shapes=[pltpu.VMEM((1, LANES), F32)],
        compiler_params=_params("arbitrary"),
    )(h, gain.reshape(1, d), router_pad)


def _dispatch_kernel(slot_ref, u_ref, xs_in_ref, xs_ref, sem, *, tt):
    del xs_in_ref
    base = pl.program_id(0) * tt

    def copy(t, kk):
        return pltpu.make_async_copy(u_ref.at[pl.ds(base + t, 1), :],
                                     xs_ref.at[pl.ds(slot_ref[0, 0, TOP_K * t + kk], 1), :], sem)

    def issue(t, carry):
        for kk in range(TOP_K):
            copy(t, kk).start()
        return carry

    def drain(t, carry):
        for kk in range(TOP_K):
            copy(t, kk).wait()
        return carry

    lax.fori_loop(0, tt, issue, 0)
    lax.fori_loop(0, tt, drain, 0)


def _dispatch(u, slots, n_slots, *, tt=ROW_TILE):
    n, d = u.shape
    return pl.pallas_call(
        functools.partial(_dispatch_kernel, tt=tt),
        grid=(n // tt,),
        in_specs=[pl.BlockSpec((1, 1, TOP_K * tt), lambda i: (i, 0, 0), memory_space=pltpu.SMEM),
                  pl.BlockSpec(memory_space=pl.ANY),
                  pl.BlockSpec(memory_space=pl.ANY)],
        out_specs=pl.BlockSpec(memory_space=pl.ANY),
        out_shape=jax.ShapeDtypeStruct((n_slots, d), u.dtype),
        scratch_shapes=[pltpu.SemaphoreType.DMA(())],
        input_output_aliases={2: 0},
        compiler_params=_params("arbitrary"),
    )(slots.reshape(n // tt, 1, TOP_K * tt), u, jnp.zeros((n_slots, d), u.dtype))


def _moe_up_kernel(be_ref, nb_ref, a_ref, wg_ref, wu_ref, o_ref):
    del be_ref

    @pl.when(pl.program_id(1) < nb_ref[0])
    def _():
        a = a_ref[...].astype(BF16)
        o_ref[...] = (_silu(_dot(a, wg_ref[0])) * _dot(a, wu_ref[0])).astype(o_ref.dtype)

    @pl.when(pl.program_id(1) >= nb_ref[0])
    def _():
        o_ref[...] = jnp.zeros_like(o_ref)


def _moe_up(xs, block_expert, n_used, wg, wu, *, tn, tm=ROW_TILE):
    ns, k = xs.shape
    n = wg.shape[2]
    rows = lambda j, i, be, nb: (jnp.minimum(i, nb[0] - 1), 0)
    wspec = pl.BlockSpec((1, k, tn), lambda j, i, be, nb: (be[i], 0, j))
    return pl.pallas_call(
        _moe_up_kernel,
        grid_spec=pltpu.PrefetchScalarGridSpec(
            num_scalar_prefetch=2,
            grid=(n // tn, ns // tm),
            in_specs=[pl.BlockSpec((tm, k), rows), wspec, wspec],
            out_specs=pl.BlockSpec((tm, tn), lambda j, i, be, nb: (i, j))),
        out_shape=jax.ShapeDtypeStruct((ns, n), BF16),
        compiler_params=_params("arbitrary", "arbitrary"),
    )(block_expert, n_used, xs, wg, wu)


def _moe_down_kernel(be_ref, nb_ref, a_ref, w_ref, o_ref):
    del be_ref

    @pl.when(pl.program_id(0) < nb_ref[0])
    def _():
        o_ref[...] = _dot(a_ref[...], w_ref[0])

    @pl.when(pl.program_id(0) >= nb_ref[0])
    def _():
        o_ref[...] = jnp.zeros_like(o_ref)


def _moe_down(hmid, block_expert, n_used, w, *, tn, tm=ROW_TILE):
    ns, k = hmid.shape
    n = w.shape[2]
    rows = lambda i, j, be, nb: (jnp.minimum(i, nb[0] - 1), 0)
    return pl.pallas_call(
        _moe_down_kernel,
        grid_spec=pltpu.PrefetchScalarGridSpec(
            num_scalar_prefetch=2,
            grid=(ns // tm, n // tn),
            in_specs=[pl.BlockSpec((tm, k), rows),
                      pl.BlockSpec((1, k, tn), lambda i, j, be, nb: (be[i], 0, j))],
            out_specs=pl.BlockSpec((tm, tn), lambda i, j, be, nb: (i, j))),
        out_shape=jax.ShapeDtypeStruct((ns, n), F32),
        compiler_params=_params("arbitrary", "arbitrary"),
    )(block_expert, n_used, hmid, w)


def _combine_kernel(slot_ref, h_ref, route_ref, g_ref, y_ref, o_ref, buf_ref, sem, *, tt):
    def copy(t, kk):
        return pltpu.make_async_copy(y_ref.at[pl.ds(slot_ref[0, 0, TOP_K * t + kk], 1), :],
                                     buf_ref.at[kk, pl.ds(t, 1), :], sem)

    def issue(t, carry):
        for kk in range(TOP_K):
            copy(t, kk).start()
        return carry

    def drain(t, carry):
        for kk in range(TOP_K):
            copy(t, kk).wait()
        return carry

    lax.fori_loop(0, tt, issue, 0)
    lax.fori_loop(0, tt, drain, 0)
    route = route_ref[...]
    x = h_ref[...] + (buf_ref[0] * route[:, TOP_K:TOP_K + 1] + buf_ref[1] * route[:, TOP_K + 1:TOP_K + 2])
    y = x * lax.rsqrt(jnp.mean(x * x, axis=-1, keepdims=True) + NORM_EPS)
    o_ref[...] = y * g_ref[...]


def _combine_norm(h, y, slots, route, gain, *, n_rows, tt=256):
    d = h.shape[1]
    return pl.pallas_call(
        functools.partial(_combine_kernel, tt=tt),
        grid=(n_rows // tt,),
        in_specs=[pl.BlockSpec((1, 1, TOP_K * tt), lambda i: (i, 0, 0), memory_space=pltpu.SMEM),
                  pl.BlockSpec((tt, d), lambda i: (i, 0)),
                  pl.BlockSpec((tt, LANES), lambda i: (i, 0)),
                  pl.BlockSpec((1, d), lambda i: (0, 0)),
                  pl.BlockSpec(memory_space=pl.ANY)],
        out_specs=pl.BlockSpec((tt, d), lambda i: (i, 0)),
        out_shape=jax.ShapeDtypeStruct((n_rows, d), F32),
        scratch_shapes=[pltpu.VMEM((TOP_K, tt, d), F32), pltpu.SemaphoreType.DMA(())],
        compiler_params=_params("arbitrary"),
    )(slots.reshape(n_rows // tt, 1, TOP_K * tt), h, route, gain.reshape(1, d), y)


def _even_layer(h, norm_mix, norm_ffn, w_in, w_out, pool_w, pool_scale, lb_logits, layer, hgrn_norm,
                w_gate, w_up, w_down, *, batch, n_real, seq):
    pool_width = pool_scale.shape[0]
    u = _rmsnorm(h, norm_mix)
    z = _matmul(u, w_in.astype(BF16), tn=1024, out_dtype=F32, n_outer=True)
    y_a = _pool_mixer(z, pool_w.astype(BF16), pool_scale, n_real=n_real, seq=seq)
    y_b = _hgrn_mixer(z, pool_width, lb_logits, layer, hgrn_norm, batch=batch, n_real=n_real, seq=seq)
    h = _matmul_residual(jnp.concatenate([y_a, y_b], axis=1), w_out.astype(BF16), h, tn=512)
    u = _rmsnorm(h, norm_ffn)
    ff_pad = -w_gate.shape[1] % 1024
    wg = jnp.pad(w_gate.astype(BF16), ((0, 0), (0, ff_pad)))
    wu = jnp.pad(w_up.astype(BF16), ((0, 0), (0, ff_pad)))
    wd = jnp.pad(w_down.astype(BF16), ((0, ff_pad), (0, 0)))
    mid = _matmul_swiglu(u, wg, wu, tn=512)
    return _matmul_residual(mid, wd, h, tn=256)


def _odd_layer(h, norm_mix, norm_ffn, final_norm, w_in, w_out, sinks, router, w_gate, w_up, w_down,
               *, batch, n_real, seq):
    d = h.shape[1]
    hd = ATT_HEAD_DIM
    n_q = sinks.shape[0]
    nq = n_q * hd
    nkv = ATT_KV_HEADS * hd
    u = _rmsnorm(h, norm_mix)
    w = w_in.astype(BF16)
    wk = w[:, nq:nq + nkv].reshape(d, ATT_KV_HEADS, hd)
    wv = w[:, nq + nkv:].reshape(d, ATT_KV_HEADS, hd)
    w_perm = jnp.concatenate([w[:, :nq], jnp.concatenate([wk, wv], axis=2).reshape(d, 2 * nkv)], axis=1)
    z = _matmul(u, w_perm, tn=1024, out_dtype=BF16, n_outer=True)
    slopes = 2.0 ** (-8.0 * jnp.arange(1, n_q + 1, dtype=F32) / n_q)
    att = _attention(z, slopes, sinks.astype(F32), batch=batch, n_real=n_real, seq=seq)
    h = _matmul_residual(att, w_out.astype(BF16), h, tn=512, m_rows=n_real)
    return _moe_ffn_norm(h, norm_ffn, final_norm, router, w_gate, w_up, w_down, n_real=n_real)


def _moe_ffn_norm(h, norm_ffn, final_norm, router, w_gate, w_up, w_down, *, n_real):
    u, route, counts = _route(h, norm_ffn, router, n_rows=n_real)
    blk = ROW_TILE
    n_blocks = n_real * TOP_K // blk + N_EXPERTS
    e_idx = route[:, 0:TOP_K].astype(jnp.int32)
    rank = route[:, 2 * TOP_K:3 * TOP_K].astype(jnp.int32)
    cnt = counts[0, :N_EXPERTS].astype(jnp.int32)
    padded = (cnt + blk - 1) // blk * blk
    pad_end = jnp.cumsum(padded)
    pad_start = pad_end - padded
    slots = pad_start[e_idx] + rank
    n_used = (pad_end[-1] // blk).astype(jnp.int32).reshape(1)
    blocks = jnp.minimum(jnp.arange(n_blocks, dtype=jnp.int32), n_used[0] - 1)
    block_expert = jnp.minimum(jnp.searchsorted(pad_end, blocks * blk, side='right'),
                               N_EXPERTS - 1).astype(jnp.int32)
    xs = _dispatch(u, slots, n_blocks * blk)
    mid = _moe_up(xs, block_expert, n_used, w_gate.astype(BF16), w_up.astype(BF16), tn=512)
    y = _moe_down(mid, block_expert, n_used, w_down.astype(BF16), tn=min(512, h.shape[1]))
    return _combine_norm(h, y, slots, route, final_norm, n_rows=n_real)


def kernel(x, meta_tokens, norm_mix, norm_ffn, final_norm, even_w_in, even_w_out, pool_w_group, pool_scale, hgrn_lb_logits, hgrn_norm, odd_w_in, odd_w_out, attn_sinks, ffn_w_gate, ffn_w_up, ffn_w_down, moe_router, moe_w_gate, moe_w_up, moe_w_down):
    batch, seq, d = x.shape
    n_real = batch * seq
    assert seq % ROW_TILE == 0 and (batch * META_BLOCK) % ROW_TILE == 0
    assert norm_mix.shape[0] == 2, "one even and one odd layer"
    meta_block = jnp.concatenate([jnp.zeros((META_BLOCK - N_META, d), F32), meta_tokens.astype(F32)], axis=0)
    h = jnp.concatenate([x.reshape(n_real, d), jnp.tile(meta_block, (batch, 1))], axis=0)
    dims = dict(batch=batch, n_real=n_real, seq=seq)
    h = _even_layer(h, norm_mix[0], norm_ffn[0], even_w_in[0], even_w_out[0], pool_w_group[0],
                    pool_scale[0], hgrn_lb_logits, 0, hgrn_norm[0], ffn_w_gate[0], ffn_w_up[0],
                    ffn_w_down[0], **dims)
    out = _odd_layer(h, norm_mix[1], norm_ffn[1], final_norm, odd_w_in[0], odd_w_out[0], attn_sinks[0],
                     moe_router[0], moe_w_gate[0], moe_w_up[0], moe_w_down[0], **dims)
    return out.reshape(batch, seq, d)
```

```python
import functools
import math

import numpy as np
import jax
import jax.numpy as jnp
from jax import lax
from jax.experimental import pallas as pl
from jax.experimental.pallas import tpu as pltpu

F32 = jnp.float32
BF16 = jnp.bfloat16

N_META = 16
NORM_EPS = 1e-5
POOL_WINDOWS = (2, 4, 8, 16)
POOL_HALO = 16
HGRN_HEAD_DIM = 128
HGRN_CHUNK = 128
HGRN_HEADS_PER_STEP = 4
ATT_HEAD_DIM = 64
ATT_KV_HEADS = 8
ATT_BLOCK = 128
N_EXPERTS = 8
TOP_K = 2

META_BLOCK = 128
ROW_TILE = 512
LANES = 128
VMEM_LIMIT = 56 * 1024 * 1024


def _params(*sem):
    return pltpu.CompilerParams(dimension_semantics=sem, vmem_limit_bytes=VMEM_LIMIT)


def _silu(x):
    return x * jax.nn.sigmoid(x)


def _dot(a, b):
    return jnp.dot(a, b, preferred_element_type=F32)


def _dot_nt(a, b):
    return lax.dot_general(a, b, (((1,), (1,)), ((), ())), preferred_element_type=F32)


def _dot_tn(a, b):
    return lax.dot_general(a, b, (((0,), (0,)), ((), ())), preferred_element_type=F32)


def _rmsnorm_kernel(h_ref, g_ref, o_ref):
    x = h_ref[...]
    y = x * lax.rsqrt(jnp.mean(x * x, axis=-1, keepdims=True) + NORM_EPS)
    o_ref[...] = (y * g_ref[...]).astype(o_ref.dtype)


def _rmsnorm(h, gain, out_dtype=BF16, tm=ROW_TILE):
    m, d = h.shape
    return pl.pallas_call(
        _rmsnorm_kernel,
        grid=(m // tm,),
        in_specs=[pl.BlockSpec((tm, d), lambda i: (i, 0)), pl.BlockSpec((1, d), lambda i: (0, 0))],
        out_specs=pl.BlockSpec((tm, d), lambda i: (i, 0)),
        out_shape=jax.ShapeDtypeStruct((m, d), out_dtype),
        compiler_params=_params("arbitrary"),
    )(h, gain.reshape(1, d))


def _mm_kernel(a_ref, w_ref, o_ref):
    o_ref[...] = _dot(a_ref[...], w_ref[...]).astype(o_ref.dtype)


def _mm_res_kernel(a_ref, w_ref, r_ref, o_ref):
    o_ref[...] = r_ref[...] + _dot(a_ref[...], w_ref[...])


def _mm_swiglu_kernel(a_ref, wg_ref, wu_ref, o_ref):
    a = a_ref[...]
    o_ref[...] = (_silu(_dot(a, wg_ref[...])) * _dot(a, wu_ref[...])).astype(o_ref.dtype)


def _matmul(a, w, *, tn, out_dtype, n_outer, tm=ROW_TILE, m_rows=None):
    m = a.shape[0] if m_rows is None else m_rows
    k = a.shape[1]
    n = w.shape[1]
    if n_outer:
        grid = (n // tn, m // tm)
        mi, ni = (lambda j, i: (i, 0)), (lambda j, i: (0, j))
        oi = lambda j, i: (i, j)
    else:
        grid = (m // tm, n // tn)
        mi, ni = (lambda i, j: (i, 0)), (lambda i, j: (0, j))
        oi = lambda i, j: (i, j)
    return pl.pallas_call(
        _mm_kernel,
        grid=grid,
        in_specs=[pl.BlockSpec((tm, k), mi), pl.BlockSpec((k, tn), ni)],
        out_specs=pl.BlockSpec((tm, tn), oi),
        out_shape=jax.ShapeDtypeStruct((m, n), out_dtype),
        compiler_params=_params("arbitrary", "arbitrary"),
    )(a, w)


def _matmul_residual(a, w, res, *, tn, tm=ROW_TILE, m_rows=None):
    m = a.shape[0] if m_rows is None else m_rows
    k = a.shape[1]
    n = w.shape[1]
    return pl.pallas_call(
        _mm_res_kernel,
        grid=(m // tm, n // tn),
        in_specs=[pl.BlockSpec((tm, k), lambda i, j: (i, 0)),
                  pl.BlockSpec((k, tn), lambda i, j: (0, j)),
                  pl.BlockSpec((tm, tn), lambda i, j: (i, j))],
        out_specs=pl.BlockSpec((tm, tn), lambda i, j: (i, j)),
        out_shape=jax.ShapeDtypeStruct((m, n), F32),
        compiler_params=_params("arbitrary", "arbitrary"),
    )(a, w, res)


def _matmul_swiglu(a, wg, wu, *, tn, tm=ROW_TILE):
    m, k = a.shape
    n = wg.shape[1]
    return pl.pallas_call(
        _mm_swiglu_kernel,
        grid=(n // tn, m // tm),
        in_specs=[pl.BlockSpec((tm, k), lambda j, i: (i, 0)),
                  pl.BlockSpec((k, tn), lambda j, i: (0, j)),
                  pl.BlockSpec((k, tn), lambda j, i: (0, j))],
        out_specs=pl.BlockSpec((tm, tn), lambda j, i: (i, j)),
        out_shape=jax.ShapeDtypeStruct((m, n), BF16),
        compiler_params=_params("arbitrary", "arbitrary"),
    )(a, wg, wu)


def _pool_kernel(z_ref, halo_ref, wg_ref, sc_ref, o_ref, *, tm, n_real, group):
    i = pl.program_id(0)
    rows = i * tm + lax.broadcasted_iota(jnp.int32, (tm, 1), 0)
    in_meta = rows >= n_real
    pos = (rows % META_BLOCK) - (META_BLOCK - N_META)
    is_pad = in_meta & (pos < 0)
    for gi, w in enumerate(POOL_WINDOWS):
        cols = slice(gi * group, (gi + 1) * group)
        x = z_ref[:, cols]
        s = jnp.concatenate([halo_ref[:, cols], x], axis=0)
        step = 1
        while step < w:
            s = s + pltpu.roll(s, step, axis=0)
            step *= 2
        count = jnp.where(in_meta, jnp.clip(pos + 1, 1, w), w).astype(F32)
        d = s[POOL_HALO:] / count - x
        y = _dot(d.astype(BF16), wg_ref[gi]) * sc_ref[:, cols]
        o_ref[:, cols] = jnp.where(is_pad, 0.0, y).astype(o_ref.dtype)


def _pool_mixer(z, w_group, scale, *, n_real, seq, tm=ROW_TILE):
    m = z.shape[0]
    n_g, group, _ = w_group.shape
    width = n_g * group
    tiles_per_seq = seq // tm
    hb = tm // POOL_HALO

    def halo_index(i):
        b = i // tiles_per_seq
        meta_tail = (n_real + b * META_BLOCK + META_BLOCK - POOL_HALO) // POOL_HALO
        real = jnp.where(i % tiles_per_seq == 0, meta_tail, i * hb - 1)
        return (jnp.where(i * tm >= n_real, n_real // POOL_HALO, real), 0)

    return pl.pallas_call(
        functools.partial(_pool_kernel, tm=tm, n_real=n_real, group=group),
        grid=(m // tm,),
        in_specs=[pl.BlockSpec((tm, width), lambda i: (i, 0)),
                  pl.BlockSpec((POOL_HALO, width), halo_index),
                  pl.BlockSpec((n_g, group, group), lambda i: (0, 0, 0)),
                  pl.BlockSpec((1, width), lambda i: (0, 0))],
        out_specs=pl.BlockSpec((tm, width), lambda i: (i, 0)),
        out_shape=jax.ShapeDtypeStruct((m, width), BF16),
        compiler_params=_params("arbitrary"),
    )(z, z, w_group, scale.reshape(1, width))


def _hgrn_tables(c):
    levels = int(math.log2(c))
    t = np.arange(c)
    blocks = [(t[None, :] <= t[:, None])]
    for l in range(1, levels + 1):
        bs, hs = 1 << l, 1 << (l - 1)
        mid = (t // bs) * bs + hs - 1
        upper = (t % bs) >= hs
        eq = upper[:, None] & (t[None, :] > mid[:, None]) & (t[None, :] <= t[:, None])
        ek = (~upper)[:, None] & (t[None, :] > t[:, None]) & (t[None, :] <= mid[:, None])
        blocks.append(eq | ek)
    sums = np.concatenate(blocks, axis=0).astype(np.float32)
    sums = np.concatenate([sums, sums], axis=1)
    x = t[:, None] ^ t[None, :]
    lev = np.where(t[None, :] > t[:, None], -1,
                   np.floor(np.log2(np.maximum(x, 1))).astype(np.int32) + (x > 0))
    return sums, lev.astype(np.int32), levels


def _hgrn_kernel(q_ref, f_ref, i_ref, g_ref, lbl_ref, gain_ref, sums_ref, lev_ref, o_ref, state_ref,
                 *, c, levels, heads, layer):
    @pl.when(pl.program_id(2) == 0)
    def _():
        state_ref[...] = jnp.zeros_like(state_ref)

    lev = lev_ref[...]
    hd = HGRN_HEAD_DIM
    head_cols = [slice(h * hd, (h + 1) * hd) for h in range(heads)]
    lbl = lbl_ref[...]
    ex = jnp.exp(lbl - lbl.max(0, keepdims=True))
    lb = ex[0:layer + 1].sum(0, keepdims=True) / ex.sum(0, keepdims=True)
    fg = lb + (1.0 - lb) * jax.nn.sigmoid(f_ref[...])
    log2_f = jnp.log2(fg)
    k = 1.0 - fg
    q = _silu(q_ref[...])
    v = i_ref[...].astype(BF16)
    hi = log2_f.astype(BF16)
    r1 = log2_f - hi.astype(F32)
    mid = r1.astype(BF16)
    lo = (r1 - mid.astype(F32)).astype(BF16)
    e = (_dot(sums_ref[...], jnp.concatenate([hi, mid], axis=0))
         + _dot(sums_ref[:, 0:c], lo))
    cum = e[0:c]
    total = cum[c - 1:c, :]
    q_dec = (q * jnp.exp2(cum)).astype(BF16)
    k_dec = (k * jnp.exp2(total - cum)).astype(BF16)
    carry = jnp.exp2(total)
    states = [state_ref[h] for h in range(heads)]
    o = [_dot_nt(q_dec[:, cs], st.astype(BF16)) for cs, st in zip(head_cols, states)]
    for h, (cs, st) in enumerate(zip(head_cols, states)):
        state_ref[h] = st * carry[:, cs] + _dot_tn(v[:, cs], k_dec[:, cs])
    q_l, k_l = q.astype(BF16), k.astype(BF16)
    scores = [jnp.where(lev == 0, _dot_nt(q_l[:, cs], k_l[:, cs]), 0.0) for cs in head_cols]
    for l in range(1, levels + 1):
        dec = jnp.exp2(e[l * c:(l + 1) * c])
        q_l, k_l = (q * dec).astype(BF16), (k * dec).astype(BF16)
        scores = [jnp.where(lev == l, _dot_nt(q_l[:, cs], k_l[:, cs]), s) for s, cs in zip(scores, head_cols)]
    o = [oh + _dot(s.astype(BF16), v[:, cs]) for oh, s, cs in zip(o, scores, head_cols)]
    o = [oh * lax.rsqrt(jnp.mean(oh * oh, axis=-1, keepdims=True) + NORM_EPS) for oh in o]
    o_ref[...] = (jnp.concatenate(o, axis=1) * gain_ref[...] * _silu(g_ref[...])).astype(o_ref.dtype)


def _hgrn_mixer(z, col0, lb_logits, layer, norm_gain, *, batch, n_real, seq):
    m = z.shape[0]
    n_lb, width = lb_logits.shape
    c = HGRN_CHUNK
    assert META_BLOCK == c
    hps = HGRN_HEADS_PER_STEP
    wb = hps * HGRN_HEAD_DIM
    n_hg = width // wb
    chunks = seq // c + 1
    sums, lev, levels = _hgrn_tables(c)

    def row_block(b, ci):
        return jnp.where(ci == 0, n_real // c + b, b * (seq // c) + ci - 1)

    def zspec(r):
        cb = (col0 + r * width) // wb
        return pl.BlockSpec((c, wb), lambda b, hg, ci: (row_block(b, ci), cb + hg))

    vec = pl.BlockSpec((1, wb), lambda b, hg, ci: (0, hg))
    return pl.pallas_call(
        functools.partial(_hgrn_kernel, c=c, levels=levels, heads=hps, layer=layer),
        grid=(batch, n_hg, chunks),
        in_specs=[zspec(0), zspec(1), zspec(2), zspec(3),
                  pl.BlockSpec((n_lb, wb), lambda b, hg, ci: (0, hg)), vec,
                  pl.BlockSpec(sums.shape, lambda b, hg, ci: (0, 0)),
                  pl.BlockSpec(lev.shape, lambda b, hg, ci: (0, 0))],
        out_specs=pl.BlockSpec((c, wb), lambda b, hg, ci: (row_block(b, ci), hg)),
        out_shape=jax.ShapeDtypeStruct((m, width), BF16),
        scratch_shapes=[pltpu.VMEM((hps, HGRN_HEAD_DIM, HGRN_HEAD_DIM), F32)],
        compiler_params=_params("arbitrary", "arbitrary", "arbitrary"),
    )(z, z, z, z, lb_logits.astype(F32), norm_gain.reshape(1, width),
      jnp.asarray(sums, BF16), jnp.asarray(lev))


def _attn_kernel(slope_ref, sink_ref, q_ref, kvc_ref, kvp_ref, kvm_ref, o_ref, *, group):
    i = pl.program_id(1)
    j = pl.program_id(2)
    hd = ATT_HEAD_DIM
    blk = ATT_BLOCK
    pairs = group // 2
    low = lax.broadcasted_iota(jnp.int32, (1, 2 * hd), 1) < hd

    def block_diag(kv):
        vk = jnp.concatenate([kv[:, hd:], kv[:, :hd]], axis=1)
        zero = jnp.zeros_like(kv)
        keys = jnp.concatenate([jnp.where(low, kv, zero), jnp.where(low, zero, vk)], axis=0)
        vals = jnp.concatenate([jnp.where(low, vk, zero), jnp.where(low, zero, kv)], axis=0)
        return keys, vals

    def with_ones(vals, n_first):
        shape = (vals.shape[0], 2 * hd)
        first = lax.broadcasted_iota(jnp.int32, shape, 0) < n_first
        ones = jnp.where(first == (lax.broadcasted_iota(jnp.int32, shape, 1) < hd), 1.0, 0.0)
        return jnp.concatenate([vals, ones.astype(BF16)], axis=1)

    k_band, v_band = block_diag(jnp.concatenate([kvp_ref[...], kvc_ref[...]], axis=0))
    k_meta, v_meta = block_diag(kvm_ref[META_BLOCK - N_META:, :])
    pad = jnp.zeros((2 * hd - 2 * N_META, 2 * hd), BF16)
    k_meta = jnp.concatenate([k_meta, pad], axis=0)
    v_meta = with_ones(jnp.concatenate([v_meta, pad], axis=0), N_META)
    v_band = with_ones(v_band, 2 * blk)
    q2 = jnp.concatenate([q_ref[:, p * 2 * hd:(p + 1) * 2 * hd] for p in range(pairs)], axis=0)
    q2 = (q2.astype(F32) * hd ** -0.5).astype(BF16)
    s = _dot_nt(q2, k_band)
    sm = _dot_nt(q2, k_meta)

    r = lax.broadcasted_iota(jnp.int32, (blk, 2 * blk), 0)
    cidx = lax.broadcasted_iota(jnp.int32, (blk, 2 * blk), 1)
    dist = r - cidx + blk
    band_ok = (dist >= 0) & (dist < blk) & ((cidx >= blk) | (i > 0))
    neg_dist = jnp.where(band_ok, -dist.astype(F32), -jnp.inf)
    neg_dist = jnp.concatenate([neg_dist] * pairs, axis=0)

    def per_row(ref, half):
        return jnp.concatenate([jnp.full((blk, 1), ref[j * group + 2 * p + half], F32) for p in range(pairs)],
                               axis=0)

    mlane = lax.broadcasted_iota(jnp.int32, (1, 2 * hd), 1)
    halves = []
    for half in range(2):
        sink = per_row(sink_ref, half)
        lb = s[:, half * 2 * blk:(half + 1) * 2 * blk] + per_row(slope_ref, half) * neg_dist
        lm = jnp.where((mlane >= half * N_META) & (mlane < (half + 1) * N_META), sm, -jnp.inf)
        mx = jnp.maximum(jnp.maximum(jnp.maximum(lb[:, :blk], lb[:, blk:]), lm).max(-1, keepdims=True), sink)
        halves.append((jnp.exp(lb - mx), jnp.exp(lm - mx), jnp.exp(sink - mx)))
    pb = jnp.concatenate([halves[0][0], halves[1][0]], axis=1).astype(BF16)
    pm = (halves[0][1] + halves[1][1]).astype(BF16)
    acc = _dot(pb, v_band) + _dot(pm, v_meta)
    o = acc[:, :2 * hd] / (acc[:, 2 * hd:] + jnp.where(low, halves[0][2], halves[1][2]))
    for p in range(pairs):
        o_ref[:, p * 2 * hd:(p + 1) * 2 * hd] = o[p * blk:(p + 1) * blk].astype(o_ref.dtype)


def _attention(z, slopes, sinks, *, batch, n_real, seq):
    hd = ATT_HEAD_DIM
    n_q = slopes.shape[0]
    group = n_q // ATT_KV_HEADS
    qw = group * hd
    kv0 = n_q * hd // (2 * hd)
    nblk = seq // ATT_BLOCK
    smem = pl.BlockSpec(memory_space=pltpu.SMEM)
    kv_spec = lambda rows: pl.BlockSpec((ATT_BLOCK, 2 * hd), lambda b, i, j: (rows(b, i), kv0 + j))
    return pl.pallas_call(
        functools.partial(_attn_kernel, group=group),
        grid=(batch, nblk, ATT_KV_HEADS),
        in_specs=[smem, smem,
                  pl.BlockSpec((ATT_BLOCK, qw), lambda b, i, j: (b * nblk + i, j)),
                  kv_spec(lambda b, i: b * nblk + i),
                  kv_spec(lambda b, i: b * nblk + jnp.maximum(i - 1, 0)),
                  kv_spec(lambda b, i: n_real // ATT_BLOCK + b)],
        out_specs=pl.BlockSpec((ATT_BLOCK, qw), lambda b, i, j: (b * nblk + i, j)),
        out_shape=jax.ShapeDtypeStruct((n_real, n_q * hd), BF16),
        compiler_params=_params("arbitrary", "arbitrary", "arbitrary"),
    )(slopes, sinks, z, z, z, z)


def _route_kernel(h_ref, g_ref, r_ref, u_ref, route_ref, count_ref, carry_ref, *, tm):
    @pl.when(pl.program_id(0) == 0)
    def _():
        carry_ref[...] = jnp.zeros_like(carry_ref)

    x = h_ref[...]
    u = x * lax.rsqrt(jnp.mean(x * x, axis=-1, keepdims=True) + NORM_EPS) * g_ref[...]
    bits = lax.bitcast_convert_type(u.astype(BF16).astype(F32), jnp.uint32)
    half = bits.shape[1] // 2
    u_ref[...] = (bits[:, :half] >> 16) | (bits[:, half:] & jnp.uint32(0xFFFF0000))
    logits = jnp.dot(u, r_ref[...], preferred_element_type=F32, precision=lax.Precision.HIGHEST)
    lane = lax.broadcasted_iota(jnp.int32, (tm, LANES), 1)
    lg = jnp.where(lane < N_EXPERTS, logits, -jnp.inf)
    m1 = lg.max(-1, keepdims=True)
    i1 = jnp.where(lg == m1, lane, LANES).min(-1, keepdims=True)
    lg2 = jnp.where(lane == i1, -jnp.inf, lg)
    m2 = lg2.max(-1, keepdims=True)
    i2 = jnp.where(lg2 == m2, lane, LANES).min(-1, keepdims=True)
    e2 = jnp.exp(m2 - m1)
    g1 = 1.0 / (1.0 + e2)
    g2 = e2 / (1.0 + e2)
    chosen = (lane == i1) | (lane == i2)
    onehot = jnp.where(chosen, 1.0, 0.0)
    rr = lax.broadcasted_iota(jnp.int32, (tm, tm), 0)
    cc = lax.broadcasted_iota(jnp.int32, (tm, tm), 1)
    before = jnp.where(cc < rr, 1.0, 0.0).astype(BF16)
    rank = _dot(before, onehot.astype(BF16)) + carry_ref[...]
    r1 = jnp.where(lane == i1, rank, 0.0).sum(-1, keepdims=True)
    r2 = jnp.where(lane == i2, rank, 0.0).sum(-1, keepdims=True)
    carry_ref[...] = carry_ref[...] + onehot.sum(0, keepdims=True)
    count_ref[...] = carry_ref[...]
    packed = jnp.where(lane == 0, i1.astype(F32), 0.0)
    for idx, val in ((1, i2.astype(F32)), (2, g1), (3, g2), (4, r1), (5, r2)):
        packed = jnp.where(lane == idx, val, packed)
    route_ref[...] = packed


def _route(h, gain, router, *, n_rows, tm=ROW_TILE):
    d = h.shape[1]
    router_pad = jnp.zeros((d, LANES), F32).at[:, :N_EXPERTS].set(router.astype(F32))
    return pl.pallas_call(
        functools.partial(_route_kernel, tm=tm),
        grid=(n_rows // tm,),
        in_specs=[pl.BlockSpec((tm, d), lambda i: (i, 0)),
                  pl.BlockSpec((1, d), lambda i: (0, 0)),
                  pl.BlockSpec((d, LANES), lambda i: (0, 0))],
        out_specs=[pl.BlockSpec((tm, d // 2), lambda i: (i, 0)),
                   pl.BlockSpec((tm, LANES), lambda i: (i, 0)),
                   pl.BlockSpec((1, LANES), lambda i: (0, 0))],
        out_shape=[jax.ShapeDtypeStruct((n_rows, d // 2), jnp.uint32),
                   jax.ShapeDtypeStruct((n_rows, LANES), F32),
                   jax.ShapeDtypeStruct((1, LANES), F32)],
        scratch_shapes=[pltpu.VMEM((1, LANES), F32)],
        compiler_params=_params("arbitrary"),
    )(h, gain.reshape(1, d), router_pad)


def _dispatch_kernel(slot_ref, u_ref, xs_in_ref, xs_ref, sem, *, tt):
    del xs_in_ref

    def copy(t, kk):
        return pltpu.make_async_copy(u_ref.at[pl.ds(t, 1), :],
                                     xs_ref.at[pl.ds(slot_ref[0, 0, TOP_K * t + kk], 1), :], sem)

    def issue(t, carry):
        for kk in range(TOP_K):
            copy(t, kk).start()
        return carry

    def drain(t, carry):
        for kk in range(TOP_K):
            copy(t, kk).wait()
        return carry

    lax.fori_loop(0, tt, issue, 0)
    lax.fori_loop(0, tt, drain, 0)


def _dispatch(u, slots, n_slots, *, tt=ROW_TILE):
    n, d = u.shape
    return pl.pallas_call(
        functools.partial(_dispatch_kernel, tt=tt),
        grid=(n // tt,),
        in_specs=[pl.BlockSpec((1, 1, TOP_K * tt), lambda i: (i, 0, 0), memory_space=pltpu.SMEM),
                  pl.BlockSpec((tt, d), lambda i: (i, 0)),
                  pl.BlockSpec(memory_space=pl.ANY)],
        out_specs=pl.BlockSpec(memory_space=pl.ANY),
        out_shape=jax.ShapeDtypeStruct((n_slots, d), u.dtype),
        scratch_shapes=[pltpu.SemaphoreType.DMA(())],
        input_output_aliases={2: 0},
        compiler_params=_params("arbitrary"),
    )(slots.reshape(n // tt, 1, TOP_K * tt), u, jnp.zeros((n_slots, d), u.dtype))


def _moe_up_kernel(be_ref, nb_ref, a_ref, wg_ref, wu_ref, o_ref):
    del be_ref

    @pl.when(pl.program_id(1) < nb_ref[0])
    def _():
        packed = a_ref[...]
        half = packed.shape[1]
        a_lo = lax.bitcast_convert_type(packed << 16, F32).astype(BF16)
        a_hi = lax.bitcast_convert_type(packed & jnp.uint32(0xFFFF0000), F32).astype(BF16)
        gate = _dot(a_lo, wg_ref[0, :half, :]) + _dot(a_hi, wg_ref[0, half:, :])
        up = _dot(a_lo, wu_ref[0, :half, :]) + _dot(a_hi, wu_ref[0, half:, :])
        o_ref[...] = (_silu(gate) * up).astype(o_ref.dtype)

    @pl.when(pl.program_id(1) >= nb_ref[0])
    def _():
        o_ref[...] = jnp.zeros_like(o_ref)


def _moe_up(xs, block_expert, n_used, wg, wu, *, tn, tm=ROW_TILE):
    ns, kp = xs.shape
    k, n = wg.shape[1:]
    assert k == 2 * kp
    rows = lambda j, i, be, nb: (jnp.minimum(i, nb[0] - 1), 0)
    wspec = pl.BlockSpec((1, k, tn), lambda j, i, be, nb: (be[i], 0, j))
    return pl.pallas_call(
        _moe_up_kernel,
        grid_spec=pltpu.PrefetchScalarGridSpec(
            num_scalar_prefetch=2,
            grid=(n // tn, ns // tm),
            in_specs=[pl.BlockSpec((tm, kp), rows), wspec, wspec],
            out_specs=pl.BlockSpec((tm, tn), lambda j, i, be, nb: (i, j))),
        out_shape=jax.ShapeDtypeStruct((ns, n), BF16),
        compiler_params=_params("arbitrary", "arbitrary"),
    )(block_expert, n_used, xs, wg, wu)


def _moe_down_kernel(be_ref, nb_ref, a_ref, w_ref, o_ref):
    del be_ref

    @pl.when(pl.program_id(0) < nb_ref[0])
    def _():
        o_ref[...] = _dot(a_ref[...], w_ref[0])

    @pl.when(pl.program_id(0) >= nb_ref[0])
    def _():
        o_ref[...] = jnp.zeros_like(o_ref)


def _moe_down(hmid, block_expert, n_used, w, *, tn, tm=ROW_TILE):
    ns, k = hmid.shape
    n = w.shape[2]
    rows = lambda i, j, be, nb: (jnp.minimum(i, nb[0] - 1), 0)
    return pl.pallas_call(
        _moe_down_kernel,
        grid_spec=pltpu.PrefetchScalarGridSpec(
            num_scalar_prefetch=2,
            grid=(ns // tm, n // tn),
            in_specs=[pl.BlockSpec((tm, k), rows),
                      pl.BlockSpec((1, k, tn), lambda i, j, be, nb: (be[i], 0, j))],
            out_specs=pl.BlockSpec((tm, tn), lambda i, j, be, nb: (i, j))),
        out_shape=jax.ShapeDtypeStruct((ns, n), F32),
        compiler_params=_params("arbitrary", "arbitrary"),
    )(block_expert, n_used, hmid, w)


def _combine_kernel(slot_ref, h_ref, route_ref, g_ref, y_ref, o_ref, buf_ref, sem, *, tt):
    def copy(t, kk):
        return pltpu.make_async_copy(y_ref.at[pl.ds(slot_ref[0, 0, TOP_K * t + kk], 1), :],
                                     buf_ref.at[kk, pl.ds(t, 1), :], sem)

    def issue(t, carry):
        for kk in range(TOP_K):
            copy(t, kk).start()
        return carry

    def drain(t, carry):
        for kk in range(TOP_K):
            copy(t, kk).wait()
        return carry

    lax.fori_loop(0, tt, issue, 0)
    lax.fori_loop(0, tt, drain, 0)
    route = route_ref[...]
    x = h_ref[...] + (buf_ref[0] * route[:, TOP_K:TOP_K + 1] + buf_ref[1] * route[:, TOP_K + 1:TOP_K + 2])
    y = x * lax.rsqrt(jnp.mean(x * x, axis=-1, keepdims=True) + NORM_EPS)
    o_ref[...] = y * g_ref[...]


def _combine_norm(h, y, slots, route, gain, *, n_rows, tt=256):
    d = h.shape[1]
    return pl.pallas_call(
        functools.partial(_combine_kernel, tt=tt),
        grid=(n_rows // tt,),
        in_specs=[pl.BlockSpec((1, 1, TOP_K * tt), lambda i: (i, 0, 0), memory_space=pltpu.SMEM),
                  pl.BlockSpec((tt, d), lambda i: (i, 0)),
                  pl.BlockSpec((tt, LANES), lambda i: (i, 0)),
                  pl.BlockSpec((1, d), lambda i: (0, 0)),
                  pl.BlockSpec(memory_space=pl.ANY)],
        out_specs=pl.BlockSpec((tt, d), lambda i: (i, 0)),
        out_shape=jax.ShapeDtypeStruct((n_rows, d), F32),
        scratch_shapes=[pltpu.VMEM((TOP_K, tt, d), F32), pltpu.SemaphoreType.DMA(())],
        compiler_params=_params("arbitrary"),
    )(slots.reshape(n_rows // tt, 1, TOP_K * tt), h, route, gain.reshape(1, d), y)


def _even_layer(h, norm_mix, norm_ffn, w_in, w_out, pool_w, pool_scale, lb_logits, layer, hgrn_norm,
                w_gate, w_up, w_down, *, batch, n_real, seq):
    pool_width = pool_scale.shape[0]
    u = _rmsnorm(h, norm_mix)
    z = _matmul(u, w_in.astype(BF16), tn=1024, out_dtype=F32, n_outer=True)
    y_a = _pool_mixer(z, pool_w.astype(BF16), pool_scale, n_real=n_real, seq=seq)
    y_b = _hgrn_mixer(z, pool_width, lb_logits, layer, hgrn_norm, batch=batch, n_real=n_real, seq=seq)
    h = _matmul_residual(jnp.concatenate([y_a, y_b], axis=1), w_out.astype(BF16), h, tn=512)
    u = _rmsnorm(h, norm_ffn)
    ff_pad = -w_gate.shape[1] % 1024
    wg = jnp.pad(w_gate.astype(BF16), ((0, 0), (0, ff_pad)))
    wu = jnp.pad(w_up.astype(BF16), ((0, 0), (0, ff_pad)))
    wd = jnp.pad(w_down.astype(BF16), ((0, ff_pad), (0, 0)))
    mid = _matmul_swiglu(u, wg, wu, tn=512)
    return _matmul_residual(mid, wd, h, tn=256)


def _odd_layer(h, norm_mix, norm_ffn, final_norm, w_in, w_out, sinks, router, w_gate, w_up, w_down,
               *, batch, n_real, seq):
    d = h.shape[1]
    hd = ATT_HEAD_DIM
    n_q = sinks.shape[0]
    nq = n_q * hd
    nkv = ATT_KV_HEADS * hd
    u = _rmsnorm(h, norm_mix)
    w = w_in.astype(BF16)
    wk = w[:, nq:nq + nkv].reshape(d, ATT_KV_HEADS, hd)
    wv = w[:, nq + nkv:].reshape(d, ATT_KV_HEADS, hd)
    w_perm = jnp.concatenate([w[:, :nq], jnp.concatenate([wk, wv], axis=2).reshape(d, 2 * nkv)], axis=1)
    z = _matmul(u, w_perm, tn=1024, out_dtype=BF16, n_outer=True)
    slopes = 2.0 ** (-8.0 * jnp.arange(1, n_q + 1, dtype=F32) / n_q)
    att = _attention(z, slopes, sinks.astype(F32), batch=batch, n_real=n_real, seq=seq)
    h = _matmul_residual(att, w_out.astype(BF16), h, tn=512, m_rows=n_real)
    return _moe_ffn_norm(h, norm_ffn, final_norm, router, w_gate, w_up, w_down, n_real=n_real)


def _moe_ffn_norm(h, norm_ffn, final_norm, router, w_gate, w_up, w_down, *, n_real):
    u, route, counts = _route(h, norm_ffn, router, n_rows=n_real)
    blk = ROW_TILE
    n_blocks = n_real * TOP_K // blk + N_EXPERTS
    e_idx = route[:, 0:TOP_K].astype(jnp.int32)
    rank = route[:, 2 * TOP_K:3 * TOP_K].astype(jnp.int32)
    cnt = counts[0, :N_EXPERTS].astype(jnp.int32)
    padded = (cnt + blk - 1) // blk * blk
    pad_end = jnp.cumsum(padded)
    pad_start = pad_end - padded
    slots = pad_start[e_idx] + rank
    n_used = (pad_end[-1] // blk).astype(jnp.int32).reshape(1)
    blocks = jnp.minimum(jnp.arange(n_blocks, dtype=jnp.int32), n_used[0] - 1)
    block_expert = jnp.minimum(jnp.searchsorted(pad_end, blocks * blk, side='right'),
                               N_EXPERTS - 1).astype(jnp.int32)
    xs = _dispatch(u, slots, n_blocks * blk)
    mid = _moe_up(xs, block_expert, n_used, w_gate.astype(BF16), w_up.astype(BF16), tn=512)
    y = _moe_down(mid, block_expert, n_used, w_down.astype(BF16), tn=min(512, h.shape[1]))
    return _combine_norm(h, y, slots, route, final_norm, n_rows=n_real)


def kernel(x, meta_tokens, norm_mix, norm_ffn, final_norm, even_w_in, even_w_out, pool_w_group, pool_scale, hgrn_lb_logits, hgrn_norm, odd_w_in, odd_w_out, attn_sinks, ffn_w_gate, ffn_w_up, ffn_w_down, moe_router, moe_w_gate, moe_w_up, moe_w_down):
    batch, seq, d = x.shape
    n_real = batch * seq
    assert seq % ROW_TILE == 0 and (batch * META_BLOCK) % ROW_TILE == 0
    assert norm_mix.shape[0] == 2, "one even and one odd layer"
    meta_block = jnp.concatenate([jnp.zeros((META_BLOCK - N_META, d), F32), meta_tokens.astype(F32)], axis=0)
    h = jnp.concatenate([x.reshape(n_real, d), jnp.tile(meta_block, (batch, 1))], axis=0)
    dims = dict(batch=batch, n_real=n_real, seq=seq)
    h = _even_layer(h, norm_mix[0], norm_ffn[0], even_w_in[0], even_w_out[0], pool_w_group[0],
                    pool_scale[0], hgrn_lb_logits, 0, hgrn_norm[0], ffn_w_gate[0], ffn_w_up[0],
                    ffn_w_down[0], **dims)
    out = _odd_layer(h, norm_mix[1], norm_ffn[1], final_norm, odd_w_in[0], odd_w_out[0], attn_sinks[0],
                     moe_router[0], moe_w_gate[0], moe_w_up[0], moe_w_down[0], **dims)
    return out.reshape(batch, seq, d)
```

```python
import functools
import math

import numpy as np
import jax
import jax.numpy as jnp
from jax import lax
from jax.experimental import pallas as pl
from jax.experimental.pallas import tpu as pltpu

F32 = jnp.float32
BF16 = jnp.bfloat16

N_META = 16
NORM_EPS = 1e-5
POOL_WINDOWS = (2, 4, 8, 16)
POOL_HALO = 16
HGRN_HEAD_DIM = 128
HGRN_CHUNK = 128
HGRN_HEADS_PER_STEP = 4
ATT_HEAD_DIM = 64
ATT_KV_HEADS = 8
ATT_BLOCK = 128
N_EXPERTS = 8
TOP_K = 2

META_BLOCK = 128
ROW_TILE = 512
LANES = 128
VMEM_LIMIT = 56 * 1024 * 1024


def _params(*sem):
    return pltpu.CompilerParams(dimension_semantics=sem, vmem_limit_bytes=VMEM_LIMIT)


def _silu(x):
    return x * jax.nn.sigmoid(x)


def _dot(a, b):
    return jnp.dot(a, b, preferred_element_type=F32)


def _dot_nt(a, b):
    return lax.dot_general(a, b, (((1,), (1,)), ((), ())), preferred_element_type=F32)


def _dot_tn(a, b):
    return lax.dot_general(a, b, (((0,), (0,)), ((), ())), preferred_element_type=F32)


def _rmsnorm_kernel(h_ref, g_ref, o_ref):
    x = h_ref[...]
    y = x * lax.rsqrt(jnp.mean(x * x, axis=-1, keepdims=True) + NORM_EPS)
    o_ref[...] = (y * g_ref[...]).astype(o_ref.dtype)


def _rmsnorm(h, gain, out_dtype=BF16, tm=ROW_TILE):
    m, d = h.shape
    return pl.pallas_call(
        _rmsnorm_kernel,
        grid=(m // tm,),
        in_specs=[pl.BlockSpec((tm, d), lambda i: (i, 0)), pl.BlockSpec((1, d), lambda i: (0, 0))],
        out_specs=pl.BlockSpec((tm, d), lambda i: (i, 0)),
        out_shape=jax.ShapeDtypeStruct((m, d), out_dtype),
        compiler_params=_params("arbitrary"),
    )(h, gain.reshape(1, d))


def _mm_kernel(a_ref, w_ref, o_ref, wb_ref):
    @pl.when(pl.program_id(1) == 0)
    def _():
        wb_ref[...] = w_ref[...].astype(BF16)

    o_ref[...] = _dot(a_ref[...], wb_ref[...]).astype(o_ref.dtype)


def _mm_res_kernel(a_ref, w_ref, r_ref, o_ref):
    o_ref[...] = r_ref[...] + _dot(a_ref[...], w_ref[...])


def _mm_swiglu_kernel(a_ref, wg_ref, wu_ref, o_ref, wgb_ref, wub_ref):
    @pl.when(pl.program_id(1) == 0)
    def _():
        wgb_ref[...] = wg_ref[...].astype(BF16)
        wub_ref[...] = wu_ref[...].astype(BF16)

    a = a_ref[...]
    o_ref[...] = (_silu(_dot(a, wgb_ref[...])) * _dot(a, wub_ref[...])).astype(o_ref.dtype)


def _matmul(a, w, *, tn, out_dtype, tm=ROW_TILE):
    m, k = a.shape
    n = w.shape[1]
    return pl.pallas_call(
        _mm_kernel,
        grid=(n // tn, m // tm),
        in_specs=[pl.BlockSpec((tm, k), lambda j, i: (i, 0)), pl.BlockSpec((k, tn), lambda j, i: (0, j))],
        out_specs=pl.BlockSpec((tm, tn), lambda j, i: (i, j)),
        out_shape=jax.ShapeDtypeStruct((m, n), out_dtype),
        scratch_shapes=[pltpu.VMEM((k, tn), BF16)],
        compiler_params=_params("arbitrary", "arbitrary"),
    )(a, w)


def _matmul_residual(a, w, res, *, tn, tm=ROW_TILE, m_rows=None):
    m = a.shape[0] if m_rows is None else m_rows
    k = a.shape[1]
    n = w.shape[1]
    return pl.pallas_call(
        _mm_res_kernel,
        grid=(m // tm, n // tn),
        in_specs=[pl.BlockSpec((tm, k), lambda i, j: (i, 0)),
                  pl.BlockSpec((k, tn), lambda i, j: (0, j)),
                  pl.BlockSpec((tm, tn), lambda i, j: (i, j))],
        out_specs=pl.BlockSpec((tm, tn), lambda i, j: (i, j)),
        out_shape=jax.ShapeDtypeStruct((m, n), F32),
        compiler_params=_params("arbitrary", "arbitrary"),
    )(a, w, res)


def _matmul_swiglu(a, wg, wu, *, tn, tm=ROW_TILE):
    m, k = a.shape
    n = wg.shape[1]
    return pl.pallas_call(
        _mm_swiglu_kernel,
        grid=(n // tn, m // tm),
        in_specs=[pl.BlockSpec((tm, k), lambda j, i: (i, 0)),
                  pl.BlockSpec((k, tn), lambda j, i: (0, j)),
                  pl.BlockSpec((k, tn), lambda j, i: (0, j))],
        out_specs=pl.BlockSpec((tm, tn), lambda j, i: (i, j)),
        out_shape=jax.ShapeDtypeStruct((m, n), BF16),
        scratch_shapes=[pltpu.VMEM((k, tn), BF16), pltpu.VMEM((k, tn), BF16)],
        compiler_params=_params("arbitrary", "arbitrary"),
    )(a, wg, wu)


def _pool_kernel(z_ref, halo_ref, wg_ref, sc_ref, o_ref, *, tm, n_real, group):
    i = pl.program_id(0)
    rows = i * tm + lax.broadcasted_iota(jnp.int32, (tm, 1), 0)
    in_meta = rows >= n_real
    pos = (rows % META_BLOCK) - (META_BLOCK - N_META)
    is_pad = in_meta & (pos < 0)
    for gi, w in enumerate(POOL_WINDOWS):
        cols = slice(gi * group, (gi + 1) * group)
        x = z_ref[:, cols]
        s = jnp.concatenate([halo_ref[:, cols], x], axis=0)
        step = 1
        while step < w:
            s = s + pltpu.roll(s, step, axis=0)
            step *= 2
        count = jnp.where(in_meta, jnp.clip(pos + 1, 1, w), w).astype(F32)
        d = s[POOL_HALO:] / count - x
        y = _dot(d.astype(BF16), wg_ref[gi]) * sc_ref[:, cols]
        o_ref[:, cols] = jnp.where(is_pad, 0.0, y).astype(o_ref.dtype)
    rest = o_ref.shape[1] - len(POOL_WINDOWS) * group
    o_ref[:, len(POOL_WINDOWS) * group:] = jnp.zeros((tm, rest), o_ref.dtype)


def _pool_mixer(z, w_group, scale, *, out_width, n_real, seq, tm=ROW_TILE):
    m = z.shape[0]
    n_g, group, _ = w_group.shape
    width = n_g * group
    tiles_per_seq = seq // tm
    hb = tm // POOL_HALO

    def halo_index(i):
        b = i // tiles_per_seq
        meta_tail = (n_real + b * META_BLOCK + META_BLOCK - POOL_HALO) // POOL_HALO
        real = jnp.where(i % tiles_per_seq == 0, meta_tail, i * hb - 1)
        return (jnp.where(i * tm >= n_real, n_real // POOL_HALO, real), 0)

    return pl.pallas_call(
        functools.partial(_pool_kernel, tm=tm, n_real=n_real, group=group),
        grid=(m // tm,),
        in_specs=[pl.BlockSpec((tm, width), lambda i: (i, 0)),
                  pl.BlockSpec((POOL_HALO, width), halo_index),
                  pl.BlockSpec((n_g, group, group), lambda i: (0, 0, 0)),
                  pl.BlockSpec((1, width), lambda i: (0, 0))],
        out_specs=pl.BlockSpec((tm, out_width), lambda i: (i, 0)),
        out_shape=jax.ShapeDtypeStruct((m, out_width), BF16),
        compiler_params=_params("arbitrary"),
    )(z, z, w_group, scale.reshape(1, width))


def _hgrn_tables(c):
    levels = int(math.log2(c))
    t = np.arange(c)
    blocks = [(t[None, :] <= t[:, None])]
    for l in range(1, levels + 1):
        bs, hs = 1 << l, 1 << (l - 1)
        mid = (t // bs) * bs + hs - 1
        upper = (t % bs) >= hs
        eq = upper[:, None] & (t[None, :] > mid[:, None]) & (t[None, :] <= t[:, None])
        ek = (~upper)[:, None] & (t[None, :] > t[:, None]) & (t[None, :] <= mid[:, None])
        blocks.append(eq | ek)
    sums = np.concatenate(blocks, axis=0).astype(np.float32)
    sums = np.concatenate([sums, sums], axis=1)
    x = t[:, None] ^ t[None, :]
    lev = np.where(t[None, :] > t[:, None], -1,
                   np.floor(np.log2(np.maximum(x, 1))).astype(np.int32) + (x > 0))
    return sums, lev.astype(np.int32), levels


def _hgrn_kernel(q_ref, f_ref, i_ref, g_ref, lbl_ref, gain_ref, sums_ref, lev_ref, y_in_ref, o_ref, state_ref,
                 *, c, levels, heads, layer):
    del y_in_ref
    @pl.when(pl.program_id(2) == 0)
    def _():
        state_ref[...] = jnp.zeros_like(state_ref)

    lev = lev_ref[...]
    hd = HGRN_HEAD_DIM
    head_cols = [slice(h * hd, (h + 1) * hd) for h in range(heads)]
    lbl = lbl_ref[...]
    ex = jnp.exp(lbl - lbl.max(0, keepdims=True))
    lb = ex[0:layer + 1].sum(0, keepdims=True) / ex.sum(0, keepdims=True)
    fg = lb + (1.0 - lb) * jax.nn.sigmoid(f_ref[...])
    log2_f = jnp.log2(fg)
    k = 1.0 - fg
    q = _silu(q_ref[...])
    v = i_ref[...].astype(BF16)
    hi = log2_f.astype(BF16)
    r1 = log2_f - hi.astype(F32)
    mid = r1.astype(BF16)
    lo = (r1 - mid.astype(F32)).astype(BF16)
    e = (_dot(sums_ref[...], jnp.concatenate([hi, mid], axis=0))
         + _dot(sums_ref[:, 0:c], lo))
    cum = e[0:c]
    total = cum[c - 1:c, :]
    q_dec = (q * jnp.exp2(cum)).astype(BF16)
    k_dec = (k * jnp.exp2(total - cum)).astype(BF16)
    carry = jnp.exp2(total)
    states = [state_ref[h] for h in range(heads)]
    o = [_dot_nt(q_dec[:, cs], st.astype(BF16)) for cs, st in zip(head_cols, states)]
    for h, (cs, st) in enumerate(zip(head_cols, states)):
        state_ref[h] = st * carry[:, cs] + _dot_tn(v[:, cs], k_dec[:, cs])
    q_l, k_l = q.astype(BF16), k.astype(BF16)
    scores = [jnp.where(lev == 0, _dot_nt(q_l[:, cs], k_l[:, cs]), 0.0) for cs in head_cols]
    for l in range(1, levels + 1):
        dec = jnp.exp2(e[l * c:(l + 1) * c])
        q_l, k_l = (q * dec).astype(BF16), (k * dec).astype(BF16)
        scores = [jnp.where(lev == l, _dot_nt(q_l[:, cs], k_l[:, cs]), s) for s, cs in zip(scores, head_cols)]
    o = [oh + _dot(s.astype(BF16), v[:, cs]) for oh, s, cs in zip(o, scores, head_cols)]
    o = [oh * lax.rsqrt(jnp.mean(oh * oh, axis=-1, keepdims=True) + NORM_EPS) for oh in o]
    o_ref[...] = (jnp.concatenate(o, axis=1) * gain_ref[...] * _silu(g_ref[...])).astype(o_ref.dtype)


def _hgrn_mixer(z, col0, lb_logits, layer, norm_gain, y, y_col0, *, batch, n_real, seq):
    m = z.shape[0]
    n_lb, width = lb_logits.shape
    c = HGRN_CHUNK
    assert META_BLOCK == c
    hps = HGRN_HEADS_PER_STEP
    wb = hps * HGRN_HEAD_DIM
    n_hg = width // wb
    chunks = seq // c + 1
    sums, lev, levels = _hgrn_tables(c)

    def row_block(b, ci):
        return jnp.where(ci == 0, n_real // c + b, b * (seq // c) + ci - 1)

    def zspec(r):
        cb = (col0 + r * width) // wb
        return pl.BlockSpec((c, wb), lambda b, hg, ci: (row_block(b, ci), cb + hg))

    vec = pl.BlockSpec((1, wb), lambda b, hg, ci: (0, hg))
    return pl.pallas_call(
        functools.partial(_hgrn_kernel, c=c, levels=levels, heads=hps, layer=layer),
        grid=(batch, n_hg, chunks),
        in_specs=[zspec(0), zspec(1), zspec(2), zspec(3),
                  pl.BlockSpec((n_lb, wb), lambda b, hg, ci: (0, hg)), vec,
                  pl.BlockSpec(sums.shape, lambda b, hg, ci: (0, 0)),
                  pl.BlockSpec(lev.shape, lambda b, hg, ci: (0, 0)),
                  pl.BlockSpec(memory_space=pl.ANY)],
        out_specs=pl.BlockSpec((c, wb), lambda b, hg, ci: (row_block(b, ci), y_col0 // wb + hg)),
        out_shape=jax.ShapeDtypeStruct(y.shape, y.dtype),
        scratch_shapes=[pltpu.VMEM((hps, HGRN_HEAD_DIM, HGRN_HEAD_DIM), F32)],
        input_output_aliases={8: 0},
        compiler_params=_params("arbitrary", "arbitrary", "arbitrary"),
    )(z, z, z, z, lb_logits.astype(F32), norm_gain.reshape(1, width),
      jnp.asarray(sums, BF16), jnp.asarray(lev), y)


def _attn_kernel(slope_ref, sink_ref, q_ref, kvc_ref, kvp_ref, kvm_ref, o_ref, *, group):
    i = pl.program_id(1)
    j = pl.program_id(2)
    hd = ATT_HEAD_DIM
    blk = ATT_BLOCK
    pairs = group // 2
    low = lax.broadcasted_iota(jnp.int32, (1, 2 * hd), 1) < hd

    def block_diag(kv):
        vk = jnp.concatenate([kv[:, hd:], kv[:, :hd]], axis=1)
        zero = jnp.zeros_like(kv)
        keys = jnp.concatenate([jnp.where(low, kv, zero), jnp.where(low, zero, vk)], axis=0)
        vals = jnp.concatenate([jnp.where(low, vk, zero), jnp.where(low, zero, kv)], axis=0)
        return keys, vals

    def with_ones(vals, n_first):
        shape = (vals.shape[0], 2 * hd)
        first = lax.broadcasted_iota(jnp.int32, shape, 0) < n_first
        ones = jnp.where(first == (lax.broadcasted_iota(jnp.int32, shape, 1) < hd), 1.0, 0.0)
        return jnp.concatenate([vals, ones.astype(BF16)], axis=1)

    k_band, v_band = block_diag(jnp.concatenate([kvp_ref[...], kvc_ref[...]], axis=0))
    k_meta, v_meta = block_diag(kvm_ref[META_BLOCK - N_META:, :])
    pad = jnp.zeros((2 * hd - 2 * N_META, 2 * hd), BF16)
    k_meta = jnp.concatenate([k_meta, pad], axis=0)
    v_meta = with_ones(jnp.concatenate([v_meta, pad], axis=0), N_META)
    v_band = with_ones(v_band, 2 * blk)
    q2 = jnp.concatenate([q_ref[:, p * 2 * hd:(p + 1) * 2 * hd] for p in range(pairs)], axis=0)
    q2 = (q2.astype(F32) * hd ** -0.5).astype(BF16)
    s = _dot_nt(q2, k_band)
    sm = _dot_nt(q2, k_meta)

    r = lax.broadcasted_iota(jnp.int32, (blk, 2 * blk), 0)
    cidx = lax.broadcasted_iota(jnp.int32, (blk, 2 * blk), 1)
    dist = r - cidx + blk
    band_ok = (dist >= 0) & (dist < blk) & ((cidx >= blk) | (i > 0))
    neg_dist = jnp.where(band_ok, -dist.astype(F32), -jnp.inf)
    neg_dist = jnp.concatenate([neg_dist] * pairs, axis=0)

    def per_row(ref, half):
        return jnp.concatenate([jnp.full((blk, 1), ref[j * group + 2 * p + half], F32) for p in range(pairs)],
                               axis=0)

    mlane = lax.broadcasted_iota(jnp.int32, (1, 2 * hd), 1)
    halves = []
    for half in range(2):
        sink = per_row(sink_ref, half)
        lb = s[:, half * 2 * blk:(half + 1) * 2 * blk] + per_row(slope_ref, half) * neg_dist
        lm = jnp.where((mlane >= half * N_META) & (mlane < (half + 1) * N_META), sm, -jnp.inf)
        mx = jnp.maximum(jnp.maximum(jnp.maximum(lb[:, :blk], lb[:, blk:]), lm).max(-1, keepdims=True), sink)
        halves.append((jnp.exp(lb - mx), jnp.exp(lm - mx), jnp.exp(sink - mx)))
    pb = jnp.concatenate([halves[0][0], halves[1][0]], axis=1).astype(BF16)
    pm = (halves[0][1] + halves[1][1]).astype(BF16)
    acc = _dot(pb, v_band) + _dot(pm, v_meta)
    o = acc[:, :2 * hd] / (acc[:, 2 * hd:] + jnp.where(low, halves[0][2], halves[1][2]))
    for p in range(pairs):
        o_ref[:, p * 2 * hd:(p + 1) * 2 * hd] = o[p * blk:(p + 1) * blk].astype(o_ref.dtype)


def _attention(z, slopes, sinks, *, batch, n_real, seq):
    hd = ATT_HEAD_DIM
    n_q = slopes.shape[0]
    group = n_q // ATT_KV_HEADS
    qw = group * hd
    kv0 = n_q * hd // (2 * hd)
    nblk = seq // ATT_BLOCK
    smem = pl.BlockSpec(memory_space=pltpu.SMEM)
    kv_spec = lambda rows: pl.BlockSpec((ATT_BLOCK, 2 * hd), lambda b, i, j: (rows(b, i), kv0 + j))
    return pl.pallas_call(
        functools.partial(_attn_kernel, group=group),
        grid=(batch, nblk, ATT_KV_HEADS),
        in_specs=[smem, smem,
                  pl.BlockSpec((ATT_BLOCK, qw), lambda b, i, j: (b * nblk + i, j)),
                  kv_spec(lambda b, i: b * nblk + i),
                  kv_spec(lambda b, i: b * nblk + jnp.maximum(i - 1, 0)),
                  kv_spec(lambda b, i: n_real // ATT_BLOCK + b)],
        out_specs=pl.BlockSpec((ATT_BLOCK, qw), lambda b, i, j: (b * nblk + i, j)),
        out_shape=jax.ShapeDtypeStruct((n_real, n_q * hd), BF16),
        compiler_params=_params("arbitrary", "arbitrary", "arbitrary"),
    )(slopes, sinks, z, z, z, z)


def _route_kernel(h_ref, g_ref, r_ref, u_ref, route_ref, count_ref, carry_ref, *, tm):
    @pl.when(pl.program_id(0) == 0)
    def _():
        carry_ref[...] = jnp.zeros_like(carry_ref)

    x = h_ref[...]
    u = x * lax.rsqrt(jnp.mean(x * x, axis=-1, keepdims=True) + NORM_EPS) * g_ref[...]
    bits = lax.bitcast_convert_type(u.astype(BF16).astype(F32), jnp.uint32)
    half = bits.shape[1] // 2
    u_ref[...] = (bits[:, :half] >> 16) | (bits[:, half:] & jnp.uint32(0xFFFF0000))
    logits = jnp.dot(u, r_ref[...], preferred_element_type=F32, precision=lax.Precision.HIGHEST)
    lane = lax.broadcasted_iota(jnp.int32, (tm, LANES), 1)
    lg = jnp.where(lane < N_EXPERTS, logits, -jnp.inf)
    m1 = lg.max(-1, keepdims=True)
    i1 = jnp.where(lg == m1, lane, LANES).min(-1, keepdims=True)
    lg2 = jnp.where(lane == i1, -jnp.inf, lg)
    m2 = lg2.max(-1, keepdims=True)
    i2 = jnp.where(lg2 == m2, lane, LANES).min(-1, keepdims=True)
    e2 = jnp.exp(m2 - m1)
    g1 = 1.0 / (1.0 + e2)
    g2 = e2 / (1.0 + e2)
    chosen = (lane == i1) | (lane == i2)
    onehot = jnp.where(chosen, 1.0, 0.0)
    rr = lax.broadcasted_iota(jnp.int32, (tm, tm), 0)
    cc = lax.broadcasted_iota(jnp.int32, (tm, tm), 1)
    before = jnp.where(cc < rr, 1.0, 0.0).astype(BF16)
    rank = _dot(before, onehot.astype(BF16)) + carry_ref[...]
    r1 = jnp.where(lane == i1, rank, 0.0).sum(-1, keepdims=True)
    r2 = jnp.where(lane == i2, rank, 0.0).sum(-1, keepdims=True)
    carry_ref[...] = carry_ref[...] + onehot.sum(0, keepdims=True)
    count_ref[...] = carry_ref[...]
    packed = jnp.where(lane == 0, i1.astype(F32), 0.0)
    for idx, val in ((1, i2.astype(F32)), (2, g1), (3, g2), (4, r1), (5, r2)):
        packed = jnp.where(lane == idx, val, packed)
    route_ref[...] = packed


def _route(h, gain, router, *, n_rows, tm=ROW_TILE):
    d = h.shape[1]
    router_pad = jnp.zeros((d, LANES), F32).at[:, :N_EXPERTS].set(router.astype(F32))
    return pl.pallas_call(
        functools.partial(_route_kernel, tm=tm),
        grid=(n_rows // tm,),
        in_specs=[pl.BlockSpec((tm, d), lambda i: (i, 0)),
                  pl.BlockSpec((1, d), lambda i: (0, 0)),
                  pl.BlockSpec((d, LANES), lambda i: (0, 0))],
        out_specs=[pl.BlockSpec((tm, d // 2), lambda i: (i, 0)),
                   pl.BlockSpec((tm, LANES), lambda i: (i, 0)),
                   pl.BlockSpec((1, LANES), lambda i: (0, 0))],
        out_shape=[jax.ShapeDtypeStruct((n_rows, d // 2), jnp.uint32),
                   jax.ShapeDtypeStruct((n_rows, LANES), F32),
                   jax.ShapeDtypeStruct((1, LANES), F32)],
        scratch_shapes=[pltpu.VMEM((1, LANES), F32)],
        compiler_params=_params("arbitrary"),
    )(h, gain.reshape(1, d), router_pad)


def _dispatch_kernel(slot_ref, u_ref, xs_in_ref, xs_ref, sem, *, tt):
    del xs_in_ref

    def copy(t, kk):
        return pltpu.make_async_copy(u_ref.at[pl.ds(t, 1), :],
                                     xs_ref.at[pl.ds(slot_ref[0, 0, TOP_K * t + kk], 1), :], sem)

    def issue(t, carry):
        for kk in range(TOP_K):
            copy(t, kk).start()
        return carry

    def drain(t, carry):
        for kk in range(TOP_K):
            copy(t, kk).wait()
        return carry

    lax.fori_loop(0, tt, issue, 0)
    lax.fori_loop(0, tt, drain, 0)


def _dispatch(u, slots, n_slots, *, tt=ROW_TILE):
    n, d = u.shape
    return pl.pallas_call(
        functools.partial(_dispatch_kernel, tt=tt),
        grid=(n // tt,),
        in_specs=[pl.BlockSpec((1, 1, TOP_K * tt), lambda i: (i, 0, 0), memory_space=pltpu.SMEM),
                  pl.BlockSpec((tt, d), lambda i: (i, 0)),
                  pl.BlockSpec(memory_space=pl.ANY)],
        out_specs=pl.BlockSpec(memory_space=pl.ANY),
        out_shape=jax.ShapeDtypeStruct((n_slots, d), u.dtype),
        scratch_shapes=[pltpu.SemaphoreType.DMA(())],
        input_output_aliases={2: 0},
        compiler_params=_params("arbitrary"),
    )(slots.reshape(n // tt, 1, TOP_K * tt), u, jnp.zeros((n_slots, d), u.dtype))


def _new_expert(be_ref, i):
    return (i == 0) | (be_ref[i] != be_ref[jnp.maximum(i - 1, 0)])


def _moe_up_kernel(be_ref, nb_ref, a_ref, wg_ref, wu_ref, o_ref, wgb_ref, wub_ref):
    i = pl.program_id(1)

    @pl.when(_new_expert(be_ref, i))
    def _():
        wgb_ref[...] = wg_ref[0].astype(BF16)
        wub_ref[...] = wu_ref[0].astype(BF16)

    @pl.when(i < nb_ref[0])
    def _():
        packed = a_ref[...]
        half = packed.shape[1]
        a_lo = lax.bitcast_convert_type(packed << 16, F32).astype(BF16)
        a_hi = lax.bitcast_convert_type(packed & jnp.uint32(0xFFFF0000), F32).astype(BF16)
        gate = _dot(a_lo, wgb_ref[:half, :]) + _dot(a_hi, wgb_ref[half:, :])
        up = _dot(a_lo, wub_ref[:half, :]) + _dot(a_hi, wub_ref[half:, :])
        o_ref[...] = (_silu(gate) * up).astype(o_ref.dtype)

    @pl.when(i >= nb_ref[0])
    def _():
        o_ref[...] = jnp.zeros_like(o_ref)


def _moe_up(xs, block_expert, n_used, wg, wu, *, tn, tm=ROW_TILE):
    ns, kp = xs.shape
    k, n = wg.shape[1:]
    assert k == 2 * kp
    rows = lambda j, i, be, nb: (jnp.minimum(i, nb[0] - 1), 0)
    wspec = pl.BlockSpec((1, k, tn), lambda j, i, be, nb: (be[i], 0, j))
    return pl.pallas_call(
        _moe_up_kernel,
        grid_spec=pltpu.PrefetchScalarGridSpec(
            num_scalar_prefetch=2,
            grid=(n // tn, ns // tm),
            in_specs=[pl.BlockSpec((tm, kp), rows), wspec, wspec],
            out_specs=pl.BlockSpec((tm, tn), lambda j, i, be, nb: (i, j)),
            scratch_shapes=[pltpu.VMEM((k, tn), BF16), pltpu.VMEM((k, tn), BF16)]),
        out_shape=jax.ShapeDtypeStruct((ns, n), BF16),
        compiler_params=_params("arbitrary", "arbitrary"),
    )(block_expert, n_used, xs, wg, wu)


def _moe_down_kernel(be_ref, nb_ref, a_ref, w_ref, o_ref, wb_ref):
    i = pl.program_id(1)

    @pl.when(_new_expert(be_ref, i))
    def _():
        wb_ref[...] = w_ref[0].astype(BF16)

    @pl.when(i < nb_ref[0])
    def _():
        o_ref[...] = _dot(a_ref[...], wb_ref[...])

    @pl.when(i >= nb_ref[0])
    def _():
        o_ref[...] = jnp.zeros_like(o_ref)


def _moe_down(hmid, block_expert, n_used, w, *, tn, tm=ROW_TILE):
    ns, k = hmid.shape
    n = w.shape[2]
    return pl.pallas_call(
        _moe_down_kernel,
        grid_spec=pltpu.PrefetchScalarGridSpec(
            num_scalar_prefetch=2,
            grid=(n // tn, ns // tm),
            in_specs=[pl.BlockSpec((tm, k), lambda j, i, be, nb: (jnp.minimum(i, nb[0] - 1), 0)),
                      pl.BlockSpec((1, k, tn), lambda j, i, be, nb: (be[i], 0, j))],
            out_specs=pl.BlockSpec((tm, tn), lambda j, i, be, nb: (i, j)),
            scratch_shapes=[pltpu.VMEM((k, tn), BF16)]),
        out_shape=jax.ShapeDtypeStruct((ns, n), F32),
        compiler_params=_params("arbitrary", "arbitrary"),
    )(block_expert, n_used, hmid, w)


def _combine_kernel(slot_ref, next_slot_ref, h_ref, route_ref, g_ref, y_ref, o_ref, buf_ref, sem, *, tt):
    i = pl.program_id(0)
    cur = i % 2

    def copy(slots, buf, t, kk):
        return pltpu.make_async_copy(y_ref.at[pl.ds(slots[0, 0, TOP_K * t + kk], 1), :],
                                     buf_ref.at[buf, kk, pl.ds(t, 1), :], sem.at[buf])

    def fetch(slots, buf):
        def body(t, carry):
            for kk in range(TOP_K):
                copy(slots, buf, t, kk).start()
            return carry
        lax.fori_loop(0, tt, body, 0)

    @pl.when(i == 0)
    def _():
        fetch(slot_ref, cur)

    @pl.when(i + 1 < pl.num_programs(0))
    def _():
        fetch(next_slot_ref, 1 - cur)

    def drain(t, carry):
        for kk in range(TOP_K):
            copy(slot_ref, cur, t, kk).wait()
        return carry

    lax.fori_loop(0, tt, drain, 0)
    route = route_ref[...]
    x = h_ref[...] + (buf_ref[cur, 0] * route[:, TOP_K:TOP_K + 1] + buf_ref[cur, 1] * route[:, TOP_K + 1:TOP_K + 2])
    y = x * lax.rsqrt(jnp.mean(x * x, axis=-1, keepdims=True) + NORM_EPS)
    o_ref[...] = y * g_ref[...]


def _combine_norm(h, y, slots, route, gain, *, n_rows, tt=256):
    d = h.shape[1]
    steps = n_rows // tt
    slots = slots.reshape(steps, 1, TOP_K * tt)
    return pl.pallas_call(
        functools.partial(_combine_kernel, tt=tt),
        grid=(steps,),
        in_specs=[pl.BlockSpec((1, 1, TOP_K * tt), lambda i: (i, 0, 0), memory_space=pltpu.SMEM),
                  pl.BlockSpec((1, 1, TOP_K * tt), lambda i: (jnp.minimum(i + 1, steps - 1), 0, 0),
                               memory_space=pltpu.SMEM),
                  pl.BlockSpec((tt, d), lambda i: (i, 0)),
                  pl.BlockSpec((tt, LANES), lambda i: (i, 0)),
                  pl.BlockSpec((1, d), lambda i: (0, 0)),
                  pl.BlockSpec(memory_space=pl.ANY)],
        out_specs=pl.BlockSpec((tt, d), lambda i: (i, 0)),
        out_shape=jax.ShapeDtypeStruct((n_rows, d), F32),
        scratch_shapes=[pltpu.VMEM((2, TOP_K, tt, d), F32), pltpu.SemaphoreType.DMA((2,))],
        compiler_params=_params("arbitrary"),
    )(slots, slots, h, route, gain.reshape(1, d), y)


def _even_layer(h, norm_mix, norm_ffn, w_in, w_out, pool_w, pool_scale, lb_logits, layer, hgrn_norm,
                w_gate, w_up, w_down, *, batch, n_real, seq):
    pool_width = pool_scale.shape[0]
    u = _rmsnorm(h, norm_mix)
    z = _matmul(u, w_in, tn=512, out_dtype=F32)
    y = _pool_mixer(z, pool_w.astype(BF16), pool_scale, out_width=w_out.shape[0], n_real=n_real, seq=seq)
    y = _hgrn_mixer(z, pool_width, lb_logits, layer, hgrn_norm, y, pool_width, batch=batch, n_real=n_real, seq=seq)
    h = _matmul_residual(y, w_out.astype(BF16), h, tn=512)
    u = _rmsnorm(h, norm_ffn)
    mid = _matmul_swiglu(u, w_gate, w_up, tn=256)
    return _matmul_residual(mid, w_down.astype(BF16), h, tn=256)


def _odd_layer(h, norm_mix, norm_ffn, final_norm, w_in, w_out, sinks, router, w_gate, w_up, w_down,
               *, batch, n_real, seq):
    d = h.shape[1]
    hd = ATT_HEAD_DIM
    n_q = sinks.shape[0]
    nq = n_q * hd
    nkv = ATT_KV_HEADS * hd
    u = _rmsnorm(h, norm_mix)
    wk = w_in[:, nq:nq + nkv].reshape(d, ATT_KV_HEADS, hd)
    wv = w_in[:, nq + nkv:].reshape(d, ATT_KV_HEADS, hd)
    w_perm = jnp.concatenate([w_in[:, :nq], jnp.concatenate([wk, wv], axis=2).reshape(d, 2 * nkv)], axis=1)
    z = _matmul(u, w_perm, tn=512, out_dtype=BF16)
    slopes = 2.0 ** (-8.0 * jnp.arange(1, n_q + 1, dtype=F32) / n_q)
    att = _attention(z, slopes, sinks.astype(F32), batch=batch, n_real=n_real, seq=seq)
    h = _matmul_residual(att, w_out.astype(BF16), h, tn=512, m_rows=n_real)
    return _moe_ffn_norm(h, norm_ffn, final_norm, router, w_gate, w_up, w_down, n_real=n_real)


def _moe_ffn_norm(h, norm_ffn, final_norm, router, w_gate, w_up, w_down, *, n_real):
    u, route, counts = _route(h, norm_ffn, router, n_rows=n_real)
    blk = ROW_TILE
    n_blocks = n_real * TOP_K // blk + N_EXPERTS
    e_idx = route[:, 0:TOP_K].astype(jnp.int32)
    rank = route[:, 2 * TOP_K:3 * TOP_K].astype(jnp.int32)
    cnt = counts[0, :N_EXPERTS].astype(jnp.int32)
    padded = (cnt + blk - 1) // blk * blk
    pad_end = jnp.cumsum(padded)
    pad_start = pad_end - padded
    slots = pad_start[e_idx] + rank
    n_used = (pad_end[-1] // blk).astype(jnp.int32).reshape(1)
    blocks = jnp.minimum(jnp.arange(n_blocks, dtype=jnp.int32), n_used[0] - 1)
    block_expert = jnp.minimum(jnp.searchsorted(pad_end, blocks * blk, side='right'),
                               N_EXPERTS - 1).astype(jnp.int32)
    xs = _dispatch(u, slots, n_blocks * blk)
    mid = _moe_up(xs, block_expert, n_used, w_gate, w_up, tn=512)
    y = _moe_down(mid, block_expert, n_used, w_down, tn=min(512, h.shape[1]))
    return _combine_norm(h, y, slots, route, final_norm, n_rows=n_real)


def kernel(x, meta_tokens, norm_mix, norm_ffn, final_norm, even_w_in, even_w_out, pool_w_group, pool_scale, hgrn_lb_logits, hgrn_norm, odd_w_in, odd_w_out, attn_sinks, ffn_w_gate, ffn_w_up, ffn_w_down, moe_router, moe_w_gate, moe_w_up, moe_w_down):
    batch, seq, d = x.shape
    n_real = batch * seq
    assert seq % ROW_TILE == 0 and (batch * META_BLOCK) % ROW_TILE == 0
    assert norm_mix.shape[0] == 2, "one even and one odd layer"
    meta_block = jnp.concatenate([jnp.zeros((META_BLOCK - N_META, d), F32), meta_tokens.astype(F32)], axis=0)
    h = jnp.concatenate([x.reshape(n_real, d), jnp.tile(meta_block, (batch, 1))], axis=0)
    dims = dict(batch=batch, n_real=n_real, seq=seq)
    h = _even_layer(h, norm_mix[0], norm_ffn[0], even_w_in[0], even_w_out[0], pool_w_group[0],
                    pool_scale[0], hgrn_lb_logits, 0, hgrn_norm[0], ffn_w_gate[0], ffn_w_up[0],
                    ffn_w_down[0], **dims)
    out = _odd_layer(h, norm_mix[1], norm_ffn[1], final_norm, odd_w_in[0], odd_w_out[0], attn_sinks[0],
                     moe_router[0], moe_w_gate[0], moe_w_up[0], moe_w_down[0], **dims)
    return out.reshape(batch, seq, d)
```

```python
import functools
import math

import numpy as np
import jax
import jax.numpy as jnp
from jax import lax
from jax.experimental import pallas as pl
from jax.experimental.pallas import tpu as pltpu

F32 = jnp.float32
BF16 = jnp.bfloat16

N_META = 16
NORM_EPS = 1e-5
POOL_WINDOWS = (2, 4, 8, 16)
POOL_HALO = 16
HGRN_HEAD_DIM = 128
HGRN_CHUNK = 128
HGRN_HEADS_PER_STEP = 8
ATT_HEAD_DIM = 64
ATT_KV_HEADS = 8
ATT_BLOCK = 128
ATT_KV_PER_STEP = 4
N_EXPERTS = 8
TOP_K = 2

META_BLOCK = 128
ROW_TILE = 512
LANES = 128
VMEM_LIMIT = 56 * 1024 * 1024


def _params(*sem):
    return pltpu.CompilerParams(dimension_semantics=sem, vmem_limit_bytes=VMEM_LIMIT)


def _silu(x):
    return x * jax.nn.sigmoid(x)


def _dot(a, b):
    return jnp.dot(a, b, preferred_element_type=F32)


def _dot_nt(a, b):
    return lax.dot_general(a, b, (((1,), (1,)), ((), ())), preferred_element_type=F32)


def _dot_tn(a, b):
    return lax.dot_general(a, b, (((0,), (0,)), ((), ())), preferred_element_type=F32)


def _row_source(h, tm, width, col):
    if not isinstance(h, tuple):
        return [h], [pl.BlockSpec((tm, width), lambda *ids: (ids[0], col(*ids)))], None
    head, tail = h
    assert tail.shape[0] == tm and head.shape[0] % tm == 0
    head_tiles = head.shape[0] // tm
    specs = [pl.BlockSpec((tm, width), lambda *ids: (jnp.minimum(ids[0], head_tiles - 1), col(*ids))),
             pl.BlockSpec((tm, width), lambda *ids: (0, col(*ids)))]
    return [head, tail], specs, head_tiles


def _read_rows(refs, head_tiles):
    if head_tiles is None:
        return refs[0][...]
    return jnp.where(pl.program_id(0) < head_tiles, refs[0][...], refs[1][...])


def _rmsnorm_kernel(*refs, head_tiles):
    *h_refs, g_ref, o_ref = refs
    x = _read_rows(h_refs, head_tiles)
    y = x * lax.rsqrt(jnp.mean(x * x, axis=-1, keepdims=True) + NORM_EPS)
    o_ref[...] = (y * g_ref[...]).astype(o_ref.dtype)


def _rmsnorm(h, gain, out_dtype=BF16, tm=ROW_TILE):
    d = gain.shape[0]
    arrays, specs, head_tiles = _row_source(h, tm, d, lambda *ids: 0)
    m = sum(a.shape[0] for a in arrays)
    return pl.pallas_call(
        functools.partial(_rmsnorm_kernel, head_tiles=head_tiles),
        grid=(m // tm,),
        in_specs=[*specs, pl.BlockSpec((1, d), lambda i: (0, 0))],
        out_specs=pl.BlockSpec((tm, d), lambda i: (i, 0)),
        out_shape=jax.ShapeDtypeStruct((m, d), out_dtype),
        compiler_params=_params("arbitrary"),
    )(*arrays, gain.reshape(1, d))


def _mm_kernel(a_ref, w_ref, o_ref):
    o_ref[...] = _dot(a_ref[...], w_ref[...]).astype(o_ref.dtype)


def _mm_res_kernel(a_ref, w_ref, *refs, head_tiles):
    *r_refs, o_ref = refs
    o_ref[...] = _read_rows(r_refs, head_tiles) + _dot(a_ref[...], w_ref[...])


def _mm_swiglu_kernel(a_ref, wg_ref, wu_ref, o_ref, wgb_ref, wub_ref):
    @pl.when(pl.program_id(1) == 0)
    def _():
        wgb_ref[...] = wg_ref[...].astype(BF16)
        wub_ref[...] = wu_ref[...].astype(BF16)

    a = a_ref[...]
    o_ref[...] = (_silu(_dot(a, wgb_ref[...])) * _dot(a, wub_ref[...])).astype(o_ref.dtype)


def _matmul(a, w, *, tn, out_dtype, tm=ROW_TILE):
    m, k = a.shape
    n = w.shape[1]
    return pl.pallas_call(
        _mm_kernel,
        grid=(n // tn, m // tm),
        in_specs=[pl.BlockSpec((tm, k), lambda j, i: (i, 0)), pl.BlockSpec((k, tn), lambda j, i: (0, j))],
        out_specs=pl.BlockSpec((tm, tn), lambda j, i: (i, j)),
        out_shape=jax.ShapeDtypeStruct((m, n), out_dtype),
        compiler_params=_params("arbitrary", "arbitrary"),
    )(a, w)


def _matmul_residual(a, w, res, *, tn, tm=ROW_TILE, m_rows=None):
    m = a.shape[0] if m_rows is None else m_rows
    k = a.shape[1]
    n = w.shape[1]
    res_arrays, res_specs, head_tiles = _row_source(res, tm, tn, lambda i, j: j)
    return pl.pallas_call(
        functools.partial(_mm_res_kernel, head_tiles=head_tiles),
        grid=(m // tm, n // tn),
        in_specs=[pl.BlockSpec((tm, k), lambda i, j: (i, 0)),
                  pl.BlockSpec((k, tn), lambda i, j: (0, j)),
                  *res_specs],
        out_specs=pl.BlockSpec((tm, tn), lambda i, j: (i, j)),
        out_shape=jax.ShapeDtypeStruct((m, n), F32),
        compiler_params=_params("arbitrary", "arbitrary"),
    )(a, w, *res_arrays)


def _matmul_swiglu(a, wg, wu, *, tn, tm=ROW_TILE):
    m, k = a.shape
    n = wg.shape[1]
    return pl.pallas_call(
        _mm_swiglu_kernel,
        grid=(n // tn, m // tm),
        in_specs=[pl.BlockSpec((tm, k), lambda j, i: (i, 0)),
                  pl.BlockSpec((k, tn), lambda j, i: (0, j)),
                  pl.BlockSpec((k, tn), lambda j, i: (0, j))],
        out_specs=pl.BlockSpec((tm, tn), lambda j, i: (i, j)),
        out_shape=jax.ShapeDtypeStruct((m, n), BF16),
        scratch_shapes=[pltpu.VMEM((k, tn), BF16), pltpu.VMEM((k, tn), BF16)],
        compiler_params=_params("arbitrary", "arbitrary"),
    )(a, wg, wu)


def _pool_kernel(z_ref, halo_ref, wg_ref, sc_ref, o_ref, *, tm, n_real, group):
    i = pl.program_id(0)
    rows = i * tm + lax.broadcasted_iota(jnp.int32, (tm, 1), 0)
    in_meta = rows >= n_real
    pos = (rows % META_BLOCK) - (META_BLOCK - N_META)
    is_pad = in_meta & (pos < 0)
    for gi, w in enumerate(POOL_WINDOWS):
        cols = slice(gi * group, (gi + 1) * group)
        x = z_ref[:, cols]
        s = jnp.concatenate([halo_ref[:, cols], x], axis=0)
        step = 1
        while step < w:
            s = s + pltpu.roll(s, step, axis=0)
            step *= 2
        count = jnp.where(in_meta, jnp.clip(pos + 1, 1, w), w).astype(F32)
        d = s[POOL_HALO:] / count - x
        y = _dot(d.astype(BF16), wg_ref[gi]) * sc_ref[:, cols]
        o_ref[:, cols] = jnp.where(is_pad, 0.0, y).astype(o_ref.dtype)
    rest = o_ref.shape[1] - len(POOL_WINDOWS) * group
    o_ref[:, len(POOL_WINDOWS) * group:] = jnp.zeros((tm, rest), o_ref.dtype)


def _pool_mixer(z, w_group, scale, *, out_width, n_real, seq, tm=ROW_TILE):
    m = z.shape[0]
    n_g, group, _ = w_group.shape
    width = n_g * group
    tiles_per_seq = seq // tm
    hb = tm // POOL_HALO

    def halo_index(i):
        b = i // tiles_per_seq
        meta_tail = (n_real + b * META_BLOCK + META_BLOCK - POOL_HALO) // POOL_HALO
        real = jnp.where(i % tiles_per_seq == 0, meta_tail, i * hb - 1)
        return (jnp.where(i * tm >= n_real, n_real // POOL_HALO, real), 0)

    return pl.pallas_call(
        functools.partial(_pool_kernel, tm=tm, n_real=n_real, group=group),
        grid=(m // tm,),
        in_specs=[pl.BlockSpec((tm, width), lambda i: (i, 0)),
                  pl.BlockSpec((POOL_HALO, width), halo_index),
                  pl.BlockSpec((n_g, group, group), lambda i: (0, 0, 0)),
                  pl.BlockSpec((1, width), lambda i: (0, 0))],
        out_specs=pl.BlockSpec((tm, out_width), lambda i: (i, 0)),
        out_shape=jax.ShapeDtypeStruct((m, out_width), BF16),
        compiler_params=_params("arbitrary"),
    )(z, z, w_group, scale.reshape(1, width))


def _hgrn_tables(c):
    levels = int(math.log2(c))
    t = np.arange(c)
    blocks = [(t[None, :] <= t[:, None])]
    for l in range(1, levels + 1):
        bs, hs = 1 << l, 1 << (l - 1)
        mid = (t // bs) * bs + hs - 1
        upper = (t % bs) >= hs
        eq = upper[:, None] & (t[None, :] > mid[:, None]) & (t[None, :] <= t[:, None])
        ek = (~upper)[:, None] & (t[None, :] > t[:, None]) & (t[None, :] <= mid[:, None])
        blocks.append(eq | ek)
    sums = np.concatenate(blocks, axis=0).astype(np.float32)
    sums = np.concatenate([sums, sums], axis=1)
    x = t[:, None] ^ t[None, :]
    lev = np.where(t[None, :] > t[:, None], -1,
                   np.floor(np.log2(np.maximum(x, 1))).astype(np.int32) + (x > 0))
    return sums, lev.astype(np.int32), levels


def _hgrn_kernel(q_ref, f_ref, i_ref, g_ref, lbl_ref, gain_ref, sums_ref, lev_ref, y_in_ref, o_ref, state_ref,
                 *, c, levels, heads, layer):
    del y_in_ref
    @pl.when(pl.program_id(2) == 0)
    def _():
        state_ref[...] = jnp.zeros_like(state_ref)

    lev = lev_ref[...]
    hd = HGRN_HEAD_DIM
    head_cols = [slice(h * hd, (h + 1) * hd) for h in range(heads)]
    lbl = lbl_ref[...]
    ex = jnp.exp(lbl - lbl.max(0, keepdims=True))
    lb = ex[0:layer + 1].sum(0, keepdims=True) / ex.sum(0, keepdims=True)
    fg = lb + (1.0 - lb) * jax.nn.sigmoid(f_ref[...])
    log2_f = jnp.log2(fg)
    k = 1.0 - fg
    q = _silu(q_ref[...])
    v = i_ref[...].astype(BF16)
    hi = log2_f.astype(BF16)
    r1 = log2_f - hi.astype(F32)
    mid = r1.astype(BF16)
    lo = (r1 - mid.astype(F32)).astype(BF16)
    e = (_dot(sums_ref[...], jnp.concatenate([hi, mid], axis=0))
         + _dot(sums_ref[:, 0:c], lo))
    cum = e[0:c]
    total = cum[c - 1:c, :]
    q_dec = (q * jnp.exp2(cum)).astype(BF16)
    k_dec = (k * jnp.exp2(total - cum)).astype(BF16)
    carry = jnp.exp2(total)
    states = [state_ref[h] for h in range(heads)]
    o = [_dot_nt(q_dec[:, cs], st.astype(BF16)) for cs, st in zip(head_cols, states)]
    for h, (cs, st) in enumerate(zip(head_cols, states)):
        state_ref[h] = st * carry[:, cs] + _dot_tn(v[:, cs], k_dec[:, cs])
    q_l, k_l = q.astype(BF16), k.astype(BF16)
    scores = [jnp.where(lev == 0, _dot_nt(q_l[:, cs], k_l[:, cs]), 0.0) for cs in head_cols]
    for l in range(1, levels + 1):
        dec = jnp.exp2(e[l * c:(l + 1) * c])
        q_l, k_l = (q * dec).astype(BF16), (k * dec).astype(BF16)
        scores = [jnp.where(lev == l, _dot_nt(q_l[:, cs], k_l[:, cs]), s) for s, cs in zip(scores, head_cols)]
    o = [oh + _dot(s.astype(BF16), v[:, cs]) for oh, s, cs in zip(o, scores, head_cols)]
    o = [oh * lax.rsqrt(jnp.mean(oh * oh, axis=-1, keepdims=True) + NORM_EPS) for oh in o]
    o_ref[...] = (jnp.concatenate(o, axis=1) * gain_ref[...] * _silu(g_ref[...])).astype(o_ref.dtype)


def _hgrn_mixer(z, col0, lb_logits, layer, norm_gain, y, y_col0, *, batch, n_real, seq):
    m = z.shape[0]
    n_lb, width = lb_logits.shape
    c = HGRN_CHUNK
    assert META_BLOCK == c
    hps = HGRN_HEADS_PER_STEP
    wb = hps * HGRN_HEAD_DIM
    n_hg = width // wb
    chunks = seq // c + 1
    sums, lev, levels = _hgrn_tables(c)

    def row_block(b, ci):
        return jnp.where(ci == 0, n_real // c + b, b * (seq // c) + ci - 1)

    def zspec(r):
        cb = (col0 + r * width) // wb
        return pl.BlockSpec((c, wb), lambda b, hg, ci: (row_block(b, ci), cb + hg))

    vec = pl.BlockSpec((1, wb), lambda b, hg, ci: (0, hg))
    return pl.pallas_call(
        functools.partial(_hgrn_kernel, c=c, levels=levels, heads=hps, layer=layer),
        grid=(batch, n_hg, chunks),
        in_specs=[zspec(0), zspec(1), zspec(2), zspec(3),
                  pl.BlockSpec((n_lb, wb), lambda b, hg, ci: (0, hg)), vec,
                  pl.BlockSpec(sums.shape, lambda b, hg, ci: (0, 0)),
                  pl.BlockSpec(lev.shape, lambda b, hg, ci: (0, 0)),
                  pl.BlockSpec(memory_space=pl.ANY)],
        out_specs=pl.BlockSpec((c, wb), lambda b, hg, ci: (row_block(b, ci), y_col0 // wb + hg)),
        out_shape=jax.ShapeDtypeStruct(y.shape, y.dtype),
        scratch_shapes=[pltpu.VMEM((hps, HGRN_HEAD_DIM, HGRN_HEAD_DIM), F32)],
        input_output_aliases={8: 0},
        compiler_params=_params("arbitrary", "arbitrary", "arbitrary"),
    )(z, z, z, z, lb_logits.astype(F32), norm_gain.reshape(1, width),
      jnp.asarray(sums, BF16), jnp.asarray(lev), y)


def _attn_kernel(slope_ref, sink_ref, q_ref, kvc_ref, kvp_ref, kvm_ref, o_ref, *, group, kvs):
    i = pl.program_id(1)
    j = pl.program_id(2)
    hd = ATT_HEAD_DIM
    blk = ATT_BLOCK
    pairs = group // 2
    blocks = kvs * pairs
    low = lax.broadcasted_iota(jnp.int32, (1, 2 * hd), 1) < hd

    def block_diag(kv):
        vk = jnp.concatenate([kv[:, hd:], kv[:, :hd]], axis=1)
        zero = jnp.zeros_like(kv)
        keys = jnp.concatenate([jnp.where(low, kv, zero), jnp.where(low, zero, vk)], axis=0)
        vals = jnp.concatenate([jnp.where(low, vk, zero), jnp.where(low, zero, kv)], axis=0)
        return keys, vals

    def with_ones(vals, n_first):
        shape = (vals.shape[0], 2 * hd)
        first = lax.broadcasted_iota(jnp.int32, shape, 0) < n_first
        ones = jnp.where(first == (lax.broadcasted_iota(jnp.int32, shape, 1) < hd), 1.0, 0.0)
        return jnp.concatenate([vals, ones.astype(BF16)], axis=1)

    pad = jnp.zeros((2 * hd - 2 * N_META, 2 * hd), BF16)
    s, sm, v_band, v_meta = [], [], [], []
    for a in range(kvs):
        cols = slice(a * 2 * hd, (a + 1) * 2 * hd)
        kb, vb = block_diag(jnp.concatenate([kvp_ref[:, cols], kvc_ref[:, cols]], axis=0))
        km, vm = block_diag(kvm_ref[META_BLOCK - N_META:, cols])
        q2 = jnp.concatenate([q_ref[:, (a * pairs + p) * 2 * hd:(a * pairs + p + 1) * 2 * hd]
                              for p in range(pairs)], axis=0)
        q2 = (q2.astype(F32) * hd ** -0.5).astype(BF16)
        s.append(_dot_nt(q2, kb))
        sm.append(_dot_nt(q2, jnp.concatenate([km, pad], axis=0)))
        v_band.append(with_ones(vb, 2 * blk))
        v_meta.append(with_ones(jnp.concatenate([vm, pad], axis=0), N_META))
    s = jnp.concatenate(s, axis=0)
    sm = jnp.concatenate(sm, axis=0)

    r = lax.broadcasted_iota(jnp.int32, (blk, 2 * blk), 0)
    cidx = lax.broadcasted_iota(jnp.int32, (blk, 2 * blk), 1)
    dist = r - cidx + blk
    band_ok = (dist >= 0) & (dist < blk) & ((cidx >= blk) | (i > 0))
    neg_dist = jnp.where(band_ok, -dist.astype(F32), -jnp.inf)
    neg_dist = jnp.concatenate([neg_dist] * blocks, axis=0)

    def per_row(ref, half):
        return jnp.concatenate([jnp.full((blk, 1), ref[j * kvs * group + 2 * p + half], F32)
                                for p in range(blocks)], axis=0)

    mlane = lax.broadcasted_iota(jnp.int32, (1, 2 * hd), 1)
    halves = []
    for half in range(2):
        sink = per_row(sink_ref, half)
        lb = s[:, half * 2 * blk:(half + 1) * 2 * blk] + per_row(slope_ref, half) * neg_dist
        lm = jnp.where((mlane >= half * N_META) & (mlane < (half + 1) * N_META), sm, -jnp.inf)
        mx = jnp.maximum(jnp.maximum(jnp.maximum(lb[:, :blk], lb[:, blk:]), lm).max(-1, keepdims=True), sink)
        halves.append((jnp.exp(lb - mx), jnp.exp(lm - mx), jnp.exp(sink - mx)))
    pb = jnp.concatenate([halves[0][0], halves[1][0]], axis=1).astype(BF16)
    pm = (halves[0][1] + halves[1][1]).astype(BF16)
    rows = [slice(a * pairs * blk, (a + 1) * pairs * blk) for a in range(kvs)]
    acc = jnp.concatenate([_dot(pb[rs], vb) + _dot(pm[rs], vm) for rs, vb, vm in zip(rows, v_band, v_meta)],
                          axis=0)
    o = acc[:, :2 * hd] / (acc[:, 2 * hd:] + jnp.where(low, halves[0][2], halves[1][2]))
    for p in range(blocks):
        o_ref[:, p * 2 * hd:(p + 1) * 2 * hd] = o[p * blk:(p + 1) * blk].astype(o_ref.dtype)


def _attention(z, slopes, sinks, *, batch, n_real, seq):
    hd = ATT_HEAD_DIM
    n_q = slopes.shape[0]
    group = n_q // ATT_KV_HEADS
    kvs = ATT_KV_PER_STEP
    qw = kvs * group * hd
    kvw = kvs * 2 * hd
    kv0 = n_q * hd // kvw
    nblk = seq // ATT_BLOCK
    smem = pl.BlockSpec(memory_space=pltpu.SMEM)
    kv_spec = lambda rows: pl.BlockSpec((ATT_BLOCK, kvw), lambda b, i, j: (rows(b, i), kv0 + j))
    return pl.pallas_call(
        functools.partial(_attn_kernel, group=group, kvs=kvs),
        grid=(batch, nblk, ATT_KV_HEADS // kvs),
        in_specs=[smem, smem,
                  pl.BlockSpec((ATT_BLOCK, qw), lambda b, i, j: (b * nblk + i, j)),
                  kv_spec(lambda b, i: b * nblk + i),
                  kv_spec(lambda b, i: b * nblk + jnp.maximum(i - 1, 0)),
                  kv_spec(lambda b, i: n_real // ATT_BLOCK + b)],
        out_specs=pl.BlockSpec((ATT_BLOCK, qw), lambda b, i, j: (b * nblk + i, j)),
        out_shape=jax.ShapeDtypeStruct((n_real, n_q * hd), BF16),
        compiler_params=_params("arbitrary", "arbitrary", "arbitrary"),
    )(slopes, sinks, z, z, z, z)


def _route_kernel(h_ref, g_ref, r_ref, u_ref, route_ref, count_ref, carry_ref, *, tm):
    @pl.when(pl.program_id(0) == 0)
    def _():
        carry_ref[...] = jnp.zeros_like(carry_ref)

    x = h_ref[...]
    u = x * lax.rsqrt(jnp.mean(x * x, axis=-1, keepdims=True) + NORM_EPS) * g_ref[...]
    bits = lax.bitcast_convert_type(u.astype(BF16).astype(F32), jnp.uint32)
    half = bits.shape[1] // 2
    u_ref[...] = (bits[:, :half] >> 16) | (bits[:, half:] & jnp.uint32(0xFFFF0000))
    logits = jnp.dot(u, r_ref[...], preferred_element_type=F32, precision=lax.Precision.HIGHEST)
    lane = lax.broadcasted_iota(jnp.int32, (tm, LANES), 1)
    lg = jnp.where(lane < N_EXPERTS, logits, -jnp.inf)
    m1 = lg.max(-1, keepdims=True)
    i1 = jnp.where(lg == m1, lane, LANES).min(-1, keepdims=True)
    lg2 = jnp.where(lane == i1, -jnp.inf, lg)
    m2 = lg2.max(-1, keepdims=True)
    i2 = jnp.where(lg2 == m2, lane, LANES).min(-1, keepdims=True)
    e2 = jnp.exp(m2 - m1)
    g1 = 1.0 / (1.0 + e2)
    g2 = e2 / (1.0 + e2)
    chosen = (lane == i1) | (lane == i2)
    onehot = jnp.where(chosen, 1.0, 0.0)
    rr = lax.broadcasted_iota(jnp.int32, (tm, tm), 0)
    cc = lax.broadcasted_iota(jnp.int32, (tm, tm), 1)
    before = jnp.where(cc < rr, 1.0, 0.0).astype(BF16)
    rank = _dot(before, onehot.astype(BF16)) + carry_ref[...]
    r1 = jnp.where(lane == i1, rank, 0.0).sum(-1, keepdims=True)
    r2 = jnp.where(lane == i2, rank, 0.0).sum(-1, keepdims=True)
    carry_ref[...] = carry_ref[...] + onehot.sum(0, keepdims=True)
    count_ref[...] = carry_ref[...]
    packed = jnp.where(lane == 0, i1.astype(F32), 0.0)
    for idx, val in ((1, i2.astype(F32)), (2, g1), (3, g2), (4, r1), (5, r2)):
        packed = jnp.where(lane == idx, val, packed)
    route_ref[...] = packed


def _route(h, gain, router, *, n_rows, tm=ROW_TILE):
    d = h.shape[1]
    router_pad = jnp.zeros((d, LANES), F32).at[:, :N_EXPERTS].set(router.astype(F32))
    return pl.pallas_call(
        functools.partial(_route_kernel, tm=tm),
        grid=(n_rows // tm,),
        in_specs=[pl.BlockSpec((tm, d), lambda i: (i, 0)),
                  pl.BlockSpec((1, d), lambda i: (0, 0)),
                  pl.BlockSpec((d, LANES), lambda i: (0, 0))],
        out_specs=[pl.BlockSpec((tm, d // 2), lambda i: (i, 0)),
                   pl.BlockSpec((tm, LANES), lambda i: (i, 0)),
                   pl.BlockSpec((1, LANES), lambda i: (0, 0))],
        out_shape=[jax.ShapeDtypeStruct((n_rows, d // 2), jnp.uint32),
                   jax.ShapeDtypeStruct((n_rows, LANES), F32),
                   jax.ShapeDtypeStruct((1, LANES), F32)],
        scratch_shapes=[pltpu.VMEM((1, LANES), F32)],
        compiler_params=_params("arbitrary"),
    )(h, gain.reshape(1, d), router_pad)


def _dispatch_kernel(slot_ref, u_ref, xs_in_ref, xs_ref, sem, *, tt):
    del xs_in_ref

    def copy(t, kk):
        return pltpu.make_async_copy(u_ref.at[pl.ds(t, 1), :],
                                     xs_ref.at[pl.ds(slot_ref[0, 0, TOP_K * t + kk], 1), :], sem)

    def issue(t, carry):
        for kk in range(TOP_K):
            copy(t, kk).start()
        return carry

    def drain(t, carry):
        for kk in range(TOP_K):
            copy(t, kk).wait()
        return carry

    lax.fori_loop(0, tt, issue, 0)
    lax.fori_loop(0, tt, drain, 0)


def _dispatch(u, slots, n_slots, *, tt=ROW_TILE):
    n, d = u.shape
    return pl.pallas_call(
        functools.partial(_dispatch_kernel, tt=tt),
        grid=(n // tt,),
        in_specs=[pl.BlockSpec((1, 1, TOP_K * tt), lambda i: (i, 0, 0), memory_space=pltpu.SMEM),
                  pl.BlockSpec((tt, d), lambda i: (i, 0)),
                  pl.BlockSpec(memory_space=pl.ANY)],
        out_specs=pl.BlockSpec(memory_space=pl.ANY),
        out_shape=jax.ShapeDtypeStruct((n_slots, d), u.dtype),
        scratch_shapes=[pltpu.SemaphoreType.DMA(())],
        input_output_aliases={2: 0},
        compiler_params=_params("arbitrary"),
    )(slots.reshape(n // tt, 1, TOP_K * tt), u, jnp.zeros((n_slots, d), u.dtype))


def _new_expert(be_ref, i):
    return (i == 0) | (be_ref[i] != be_ref[jnp.maximum(i - 1, 0)])


def _moe_up_kernel(be_ref, nb_ref, a_ref, wg_ref, wu_ref, o_ref, wgb_ref, wub_ref):
    i = pl.program_id(1)

    @pl.when(_new_expert(be_ref, i))
    def _():
        wgb_ref[...] = wg_ref[0].astype(BF16)
        wub_ref[...] = wu_ref[0].astype(BF16)

    @pl.when(i < nb_ref[0])
    def _():
        packed = a_ref[...]
        half = packed.shape[1]
        a_lo = lax.bitcast_convert_type(packed << 16, F32).astype(BF16)
        a_hi = lax.bitcast_convert_type(packed & jnp.uint32(0xFFFF0000), F32).astype(BF16)
        gate = _dot(a_lo, wgb_ref[:half, :]) + _dot(a_hi, wgb_ref[half:, :])
        up = _dot(a_lo, wub_ref[:half, :]) + _dot(a_hi, wub_ref[half:, :])
        o_ref[...] = (_silu(gate) * up).astype(o_ref.dtype)

    @pl.when(i >= nb_ref[0])
    def _():
        o_ref[...] = jnp.zeros_like(o_ref)


def _moe_up(xs, block_expert, n_used, wg, wu, *, tn, tm=ROW_TILE):
    ns, kp = xs.shape
    k, n = wg.shape[1:]
    assert k == 2 * kp
    rows = lambda j, i, be, nb: (jnp.minimum(i, nb[0] - 1), 0)
    wspec = pl.BlockSpec((1, k, tn), lambda j, i, be, nb: (be[i], 0, j))
    return pl.pallas_call(
        _moe_up_kernel,
        grid_spec=pltpu.PrefetchScalarGridSpec(
            num_scalar_prefetch=2,
            grid=(n // tn, ns // tm),
            in_specs=[pl.BlockSpec((tm, kp), rows), wspec, wspec],
            out_specs=pl.BlockSpec((tm, tn), lambda j, i, be, nb: (i, j)),
            scratch_shapes=[pltpu.VMEM((k, tn), BF16), pltpu.VMEM((k, tn), BF16)]),
        out_shape=jax.ShapeDtypeStruct((ns, n), BF16),
        compiler_params=_params("arbitrary", "arbitrary"),
    )(block_expert, n_used, xs, wg, wu)


def _moe_down_kernel(be_ref, nb_ref, a_ref, w_ref, o_ref, wb_ref):
    i = pl.program_id(1)

    @pl.when(_new_expert(be_ref, i))
    def _():
        wb_ref[...] = w_ref[0].astype(BF16)

    @pl.when(i < nb_ref[0])
    def _():
        o_ref[...] = _dot(a_ref[...], wb_ref[...])

    @pl.when(i >= nb_ref[0])
    def _():
        o_ref[...] = jnp.zeros_like(o_ref)


def _moe_down(hmid, block_expert, n_used, w, *, tn, tm=ROW_TILE):
    ns, k = hmid.shape
    n = w.shape[2]
    return pl.pallas_call(
        _moe_down_kernel,
        grid_spec=pltpu.PrefetchScalarGridSpec(
            num_scalar_prefetch=2,
            grid=(n // tn, ns // tm),
            in_specs=[pl.BlockSpec((tm, k), lambda j, i, be, nb: (jnp.minimum(i, nb[0] - 1), 0)),
                      pl.BlockSpec((1, k, tn), lambda j, i, be, nb: (be[i], 0, j))],
            out_specs=pl.BlockSpec((tm, tn), lambda j, i, be, nb: (i, j)),
            scratch_shapes=[pltpu.VMEM((k, tn), BF16)]),
        out_shape=jax.ShapeDtypeStruct((ns, n), F32),
        compiler_params=_params("arbitrary", "arbitrary"),
    )(block_expert, n_used, hmid, w)


def _combine_kernel(slot_ref, next_slot_ref, h_ref, route_ref, g_ref, y_ref, o_ref, buf_ref, sem, *, tt):
    i = pl.program_id(0)
    cur = i % 2

    def copy(slots, buf, t, kk):
        return pltpu.make_async_copy(y_ref.at[pl.ds(slots[0, 0, TOP_K * t + kk], 1), :],
                                     buf_ref.at[buf, kk, pl.ds(t, 1), :], sem.at[buf])

    def fetch(slots, buf):
        def body(t, carry):
            for kk in range(TOP_K):
                copy(slots, buf, t, kk).start()
            return carry
        lax.fori_loop(0, tt, body, 0)

    @pl.when(i == 0)
    def _():
        fetch(slot_ref, cur)

    @pl.when(i + 1 < pl.num_programs(0))
    def _():
        fetch(next_slot_ref, 1 - cur)

    def drain(t, carry):
        for kk in range(TOP_K):
            copy(slot_ref, cur, t, kk).wait()
        return carry

    lax.fori_loop(0, tt, drain, 0)
    route = route_ref[...]
    x = h_ref[...] + (buf_ref[cur, 0] * route[:, TOP_K:TOP_K + 1] + buf_ref[cur, 1] * route[:, TOP_K + 1:TOP_K + 2])
    y = x * lax.rsqrt(jnp.mean(x * x, axis=-1, keepdims=True) + NORM_EPS)
    o_ref[...] = y * g_ref[...]


def _combine_norm(h, y, slots, route, gain, *, n_rows, tt=256):
    d = h.shape[1]
    steps = n_rows // tt
    slots = slots.reshape(steps, 1, TOP_K * tt)
    return pl.pallas_call(
        functools.partial(_combine_kernel, tt=tt),
        grid=(steps,),
        in_specs=[pl.BlockSpec((1, 1, TOP_K * tt), lambda i: (i, 0, 0), memory_space=pltpu.SMEM),
                  pl.BlockSpec((1, 1, TOP_K * tt), lambda i: (jnp.minimum(i + 1, steps - 1), 0, 0),
                               memory_space=pltpu.SMEM),
                  pl.BlockSpec((tt, d), lambda i: (i, 0)),
                  pl.BlockSpec((tt, LANES), lambda i: (i, 0)),
                  pl.BlockSpec((1, d), lambda i: (0, 0)),
                  pl.BlockSpec(memory_space=pl.ANY)],
        out_specs=pl.BlockSpec((tt, d), lambda i: (i, 0)),
        out_shape=jax.ShapeDtypeStruct((n_rows, d), F32),
        scratch_shapes=[pltpu.VMEM((2, TOP_K, tt, d), F32), pltpu.SemaphoreType.DMA((2,))],
        compiler_params=_params("arbitrary"),
    )(slots, slots, h, route, gain.reshape(1, d), y)


def _even_layer(h, norm_mix, norm_ffn, w_in, w_out, pool_w, pool_scale, lb_logits, layer, hgrn_norm,
                w_gate, w_up, w_down, *, batch, n_real, seq):
    pool_width = pool_scale.shape[0]
    u = _rmsnorm(h, norm_mix)
    z = _matmul(u, w_in.astype(BF16), tn=1024, out_dtype=F32)
    y = _pool_mixer(z, pool_w.astype(BF16), pool_scale, out_width=w_out.shape[0], n_real=n_real, seq=seq)
    y = _hgrn_mixer(z, pool_width, lb_logits, layer, hgrn_norm, y, pool_width, batch=batch, n_real=n_real, seq=seq)
    h = _matmul_residual(y, w_out.astype(BF16), h, tn=512)
    u = _rmsnorm(h, norm_ffn)
    mid = _matmul_swiglu(u, w_gate, w_up, tn=256, tm=3 * ROW_TILE)
    return _matmul_residual(mid, w_down.astype(BF16), h, tn=512)


def _odd_layer(h, norm_mix, norm_ffn, final_norm, w_in, w_out, sinks, router, w_gate, w_up, w_down,
               *, batch, n_real, seq):
    d = h.shape[1]
    hd = ATT_HEAD_DIM
    n_q = sinks.shape[0]
    nq = n_q * hd
    nkv = ATT_KV_HEADS * hd
    u = _rmsnorm(h, norm_mix)
    w = w_in.astype(BF16)
    wk = w[:, nq:nq + nkv].reshape(d, ATT_KV_HEADS, hd)
    wv = w[:, nq + nkv:].reshape(d, ATT_KV_HEADS, hd)
    w_perm = jnp.concatenate([w[:, :nq], jnp.concatenate([wk, wv], axis=2).reshape(d, 2 * nkv)], axis=1)
    z = _matmul(u, w_perm, tn=1024, out_dtype=BF16)
    slopes = 2.0 ** (-8.0 * jnp.arange(1, n_q + 1, dtype=F32) / n_q)
    att = _attention(z, slopes, sinks.astype(F32), batch=batch, n_real=n_real, seq=seq)
    h = _matmul_residual(att, w_out.astype(BF16), h, tn=512, m_rows=n_real)
    return _moe_ffn_norm(h, norm_ffn, final_norm, router, w_gate, w_up, w_down, n_real=n_real)


def _moe_ffn_norm(h, norm_ffn, final_norm, router, w_gate, w_up, w_down, *, n_real):
    u, route, counts = _route(h, norm_ffn, router, n_rows=n_real)
    blk = ROW_TILE
    n_blocks = n_real * TOP_K // blk + N_EXPERTS
    e_idx = route[:, 0:TOP_K].astype(jnp.int32)
    rank = route[:, 2 * TOP_K:3 * TOP_K].astype(jnp.int32)
    cnt = counts[0, :N_EXPERTS].astype(jnp.int32)
    padded = (cnt + blk - 1) // blk * blk
    pad_end = jnp.cumsum(padded)
    pad_start = pad_end - padded
    slots = pad_start[e_idx] + rank
    n_used = (pad_end[-1] // blk).astype(jnp.int32).reshape(1)
    blocks = jnp.minimum(jnp.arange(n_blocks, dtype=jnp.int32), n_used[0] - 1)
    block_expert = jnp.minimum(jnp.searchsorted(pad_end, blocks * blk, side='right'),
                               N_EXPERTS - 1).astype(jnp.int32)
    xs = _dispatch(u, slots, n_blocks * blk)
    mid = _moe_up(xs, block_expert, n_used, w_gate, w_up, tn=512)
    y = _moe_down(mid, block_expert, n_used, w_down, tn=min(512, h.shape[1]))
    return _combine_norm(h, y, slots, route, final_norm, n_rows=n_real)


def kernel(x, meta_tokens, norm_mix, norm_ffn, final_norm, even_w_in, even_w_out, pool_w_group, pool_scale, hgrn_lb_logits, hgrn_norm, odd_w_in, odd_w_out, attn_sinks, ffn_w_gate, ffn_w_up, ffn_w_down, moe_router, moe_w_gate, moe_w_up, moe_w_down):
    batch, seq, d = x.shape
    n_real = batch * seq
    assert seq % ROW_TILE == 0 and (batch * META_BLOCK) % ROW_TILE == 0
    assert norm_mix.shape[0] == 2, "one even and one odd layer"
    meta_block = jnp.concatenate([jnp.zeros((META_BLOCK - N_META, d), F32), meta_tokens.astype(F32)], axis=0)
    h = (x.reshape(n_real, d), jnp.tile(meta_block, (batch, 1)))
    dims = dict(batch=batch, n_real=n_real, seq=seq)
    h = _even_layer(h, norm_mix[0], norm_ffn[0], even_w_in[0], even_w_out[0], pool_w_group[0],
                    pool_scale[0], hgrn_lb_logits, 0, hgrn_norm[0], ffn_w_gate[0], ffn_w_up[0],
                    ffn_w_down[0], **dims)
    out = _odd_layer(h, norm_mix[1], norm_ffn[1], final_norm, odd_w_in[0], odd_w_out[0], attn_sinks[0],
                     moe_router[0], moe_w_gate[0], moe_w_up[0], moe_w_down[0], **dims)
    return out.reshape(batch, seq, d)
```

```python
import functools
import math

import numpy as np
import jax
import jax.numpy as jnp
from jax import lax
from jax.experimental import pallas as pl
from jax.experimental.pallas import tpu as pltpu

F32 = jnp.float32
BF16 = jnp.bfloat16

N_META = 16
NORM_EPS = 1e-5
POOL_WINDOWS = (2, 4, 8, 16)
POOL_HALO = 16
HGRN_HEAD_DIM = 128
HGRN_CHUNK = 128
HGRN_HEADS_PER_STEP = 8
HGRN_TABLE_LEVELS = 3
ATT_HEAD_DIM = 64
ATT_KV_HEADS = 8
ATT_BLOCK = 128
ATT_KV_PER_STEP = 4
N_EXPERTS = 8
TOP_K = 2

META_BLOCK = 128
ROW_TILE = 512
LANES = 128
VMEM_LIMIT = 56 * 1024 * 1024


def _params(*sem):
    return pltpu.CompilerParams(dimension_semantics=sem, vmem_limit_bytes=VMEM_LIMIT)


def _silu(x):
    return x * jax.nn.sigmoid(x)


def _dot(a, b):
    return jnp.dot(a, b, preferred_element_type=F32)


def _dot_nt(a, b):
    return lax.dot_general(a, b, (((1,), (1,)), ((), ())), preferred_element_type=F32)


def _dot_tn(a, b):
    return lax.dot_general(a, b, (((0,), (0,)), ((), ())), preferred_element_type=F32)


def _row_source(h, tm, width, col):
    if not isinstance(h, tuple):
        return [h], [pl.BlockSpec((tm, width), lambda *ids: (ids[0], col(*ids)))], None
    head, tail = h
    assert tail.shape[0] == tm and head.shape[0] % tm == 0
    head_tiles = head.shape[0] // tm
    specs = [pl.BlockSpec((tm, width), lambda *ids: (jnp.minimum(ids[0], head_tiles - 1), col(*ids))),
             pl.BlockSpec((tm, width), lambda *ids: (0, col(*ids)))]
    return [head, tail], specs, head_tiles


def _read_rows(refs, head_tiles):
    if head_tiles is None:
        return refs[0][...]
    return jnp.where(pl.program_id(0) < head_tiles, refs[0][...], refs[1][...])


def _rmsnorm_kernel(*refs, head_tiles):
    *h_refs, g_ref, o_ref = refs
    x = _read_rows(h_refs, head_tiles)
    y = x * lax.rsqrt(jnp.mean(x * x, axis=-1, keepdims=True) + NORM_EPS)
    o_ref[...] = (y * g_ref[...]).astype(o_ref.dtype)


def _rmsnorm(h, gain, out_dtype=BF16, tm=ROW_TILE):
    d = gain.shape[0]
    arrays, specs, head_tiles = _row_source(h, tm, d, lambda *ids: 0)
    m = sum(a.shape[0] for a in arrays)
    return pl.pallas_call(
        functools.partial(_rmsnorm_kernel, head_tiles=head_tiles),
        grid=(m // tm,),
        in_specs=[*specs, pl.BlockSpec((1, d), lambda i: (0, 0))],
        out_specs=pl.BlockSpec((tm, d), lambda i: (i, 0)),
        out_shape=jax.ShapeDtypeStruct((m, d), out_dtype),
        compiler_params=_params("arbitrary"),
    )(*arrays, gain.reshape(1, d))


def _mm_kernel(a_ref, w_ref, o_ref):
    o_ref[...] = _dot(a_ref[...], w_ref[...]).astype(o_ref.dtype)


def _mm_res_kernel(a_ref, w_ref, *refs, head_tiles):
    *r_refs, o_ref = refs
    o_ref[...] = _read_rows(r_refs, head_tiles) + _dot(a_ref[...], w_ref[...])


def _mm_swiglu_kernel(a_ref, wg_ref, wu_ref, o_ref, wgb_ref, wub_ref):
    @pl.when(pl.program_id(1) == 0)
    def _():
        wgb_ref[...] = wg_ref[...].astype(BF16)
        wub_ref[...] = wu_ref[...].astype(BF16)

    a = a_ref[...]
    o_ref[...] = (_silu(_dot(a, wgb_ref[...])) * _dot(a, wub_ref[...])).astype(o_ref.dtype)


def _matmul(a, w, *, tn, out_dtype, tm=ROW_TILE):
    m, k = a.shape
    n = w.shape[1]
    return pl.pallas_call(
        _mm_kernel,
        grid=(n // tn, m // tm),
        in_specs=[pl.BlockSpec((tm, k), lambda j, i: (i, 0)), pl.BlockSpec((k, tn), lambda j, i: (0, j))],
        out_specs=pl.BlockSpec((tm, tn), lambda j, i: (i, j)),
        out_shape=jax.ShapeDtypeStruct((m, n), out_dtype),
        compiler_params=_params("arbitrary", "arbitrary"),
    )(a, w)


def _matmul_residual(a, w, res, *, tn, tm=ROW_TILE, m_rows=None):
    m = a.shape[0] if m_rows is None else m_rows
    k = a.shape[1]
    n = w.shape[1]
    res_arrays, res_specs, head_tiles = _row_source(res, tm, tn, lambda i, j: j)
    return pl.pallas_call(
        functools.partial(_mm_res_kernel, head_tiles=head_tiles),
        grid=(m // tm, n // tn),
        in_specs=[pl.BlockSpec((tm, k), lambda i, j: (i, 0)),
                  pl.BlockSpec((k, tn), lambda i, j: (0, j)),
                  *res_specs],
        out_specs=pl.BlockSpec((tm, tn), lambda i, j: (i, j)),
        out_shape=jax.ShapeDtypeStruct((m, n), F32),
        compiler_params=_params("arbitrary", "arbitrary"),
    )(a, w, *res_arrays)


def _matmul_swiglu(a, wg, wu, *, tn, tm=ROW_TILE):
    m, k = a.shape
    n = wg.shape[1]
    return pl.pallas_call(
        _mm_swiglu_kernel,
        grid=(n // tn, m // tm),
        in_specs=[pl.BlockSpec((tm, k), lambda j, i: (i, 0)),
                  pl.BlockSpec((k, tn), lambda j, i: (0, j)),
                  pl.BlockSpec((k, tn), lambda j, i: (0, j))],
        out_specs=pl.BlockSpec((tm, tn), lambda j, i: (i, j)),
        out_shape=jax.ShapeDtypeStruct((m, n), BF16),
        scratch_shapes=[pltpu.VMEM((k, tn), BF16), pltpu.VMEM((k, tn), BF16)],
        compiler_params=_params("arbitrary", "arbitrary"),
    )(a, wg, wu)


def _pool_kernel(z_ref, halo_ref, wg_ref, sc_ref, o_ref, *, tm, n_real, group):
    i = pl.program_id(0)
    rows = i * tm + lax.broadcasted_iota(jnp.int32, (tm, 1), 0)
    in_meta = rows >= n_real
    pos = (rows % META_BLOCK) - (META_BLOCK - N_META)
    is_pad = in_meta & (pos < 0)
    for gi, w in enumerate(POOL_WINDOWS):
        cols = slice(gi * group, (gi + 1) * group)
        x = z_ref[:, cols]
        s = jnp.concatenate([halo_ref[:, cols], x], axis=0)
        step = 1
        while step < w:
            s = s + pltpu.roll(s, step, axis=0)
            step *= 2
        count = jnp.where(in_meta, jnp.clip(pos + 1, 1, w), w).astype(F32)
        d = s[POOL_HALO:] / count - x
        y = _dot(d.astype(BF16), wg_ref[gi]) * sc_ref[:, cols]
        o_ref[:, cols] = jnp.where(is_pad, 0.0, y).astype(o_ref.dtype)
    rest = o_ref.shape[1] - len(POOL_WINDOWS) * group
    o_ref[:, len(POOL_WINDOWS) * group:] = jnp.zeros((tm, rest), o_ref.dtype)


def _pool_mixer(z, w_group, scale, *, out_width, n_real, seq, tm=ROW_TILE):
    m = z.shape[0]
    n_g, group, _ = w_group.shape
    width = n_g * group
    tiles_per_seq = seq // tm
    hb = tm // POOL_HALO

    def halo_index(i):
        b = i // tiles_per_seq
        meta_tail = (n_real + b * META_BLOCK + META_BLOCK - POOL_HALO) // POOL_HALO
        real = jnp.where(i % tiles_per_seq == 0, meta_tail, i * hb - 1)
        return (jnp.where(i * tm >= n_real, n_real // POOL_HALO, real), 0)

    return pl.pallas_call(
        functools.partial(_pool_kernel, tm=tm, n_real=n_real, group=group),
        grid=(m // tm,),
        in_specs=[pl.BlockSpec((tm, width), lambda i: (i, 0)),
                  pl.BlockSpec((POOL_HALO, width), halo_index),
                  pl.BlockSpec((n_g, group, group), lambda i: (0, 0, 0)),
                  pl.BlockSpec((1, width), lambda i: (0, 0))],
        out_specs=pl.BlockSpec((tm, out_width), lambda i: (i, 0)),
        out_shape=jax.ShapeDtypeStruct((m, out_width), BF16),
        compiler_params=_params("arbitrary"),
    )(z, z, w_group, scale.reshape(1, width))


def _hgrn_tables(c):
    levels = int(math.log2(c))
    t = np.arange(c)
    blocks = [(t[None, :] <= t[:, None])]
    for l in range(1, HGRN_TABLE_LEVELS + 1):
        bs, hs = 1 << l, 1 << (l - 1)
        mid = (t // bs) * bs + hs - 1
        upper = (t % bs) >= hs
        eq = upper[:, None] & (t[None, :] > mid[:, None]) & (t[None, :] <= t[:, None])
        ek = (~upper)[:, None] & (t[None, :] > t[:, None]) & (t[None, :] <= mid[:, None])
        blocks.append(eq | ek)
    sums = np.concatenate(blocks, axis=0).astype(np.float32)
    sums = np.concatenate([sums, sums], axis=1)
    x = t[:, None] ^ t[None, :]
    lev = np.where(t[None, :] > t[:, None], -1,
                   np.floor(np.log2(np.maximum(x, 1))).astype(np.int32) + (x > 0))
    return sums, lev.astype(np.int32), levels


def _hgrn_kernel(q_ref, f_ref, i_ref, g_ref, lbl_ref, gain_ref, sums_ref, lev_ref, y_in_ref, o_ref, state_ref,
                 *, c, levels, heads, layer):
    del y_in_ref
    @pl.when(pl.program_id(2) == 0)
    def _():
        state_ref[...] = jnp.zeros_like(state_ref)

    lev = lev_ref[...]
    hd = HGRN_HEAD_DIM
    head_cols = [slice(h * hd, (h + 1) * hd) for h in range(heads)]
    lbl = lbl_ref[...]
    ex = jnp.exp(lbl - lbl.max(0, keepdims=True))
    lb = ex[0:layer + 1].sum(0, keepdims=True) / ex.sum(0, keepdims=True)
    fg = lb + (1.0 - lb) * jax.nn.sigmoid(f_ref[...])
    log2_f = jnp.log2(fg)
    k = 1.0 - fg
    q = _silu(q_ref[...])
    v = i_ref[...].astype(BF16)
    hi = log2_f.astype(BF16)
    r1 = log2_f - hi.astype(F32)
    mid = r1.astype(BF16)
    lo = (r1 - mid.astype(F32)).astype(BF16)
    e = (_dot(sums_ref[...], jnp.concatenate([hi, mid], axis=0))
         + _dot(sums_ref[:, 0:c], lo))
    cum = e[0:c]
    total = cum[c - 1:c, :]
    q_dec = (q * jnp.exp2(cum)).astype(BF16)
    k_dec = (k * jnp.exp2(total - cum)).astype(BF16)
    carry = jnp.exp2(total)
    states = [state_ref[h] for h in range(heads)]
    o = [_dot_nt(q_dec[:, cs], st.astype(BF16)) for cs, st in zip(head_cols, states)]
    for h, (cs, st) in enumerate(zip(head_cols, states)):
        state_ref[h] = st * carry[:, cs] + _dot_tn(v[:, cs], k_dec[:, cs])
    q_l, k_l = q.astype(BF16), k.astype(BF16)
    scores = [jnp.where(lev == 0, _dot_nt(q_l[:, cs], k_l[:, cs]), 0.0) for cs in head_cols]
    def level_sums(l):
        if l <= HGRN_TABLE_LEVELS:
            return e[l * c:(l + 1) * c]
        bs, hs = 1 << l, 1 << (l - 1)
        parts = []
        for start in range(0, c, bs):
            middle = cum[start + hs - 1:start + hs, :]
            parts += [middle - cum[start:start + hs], cum[start + hs:start + bs] - middle]
        return jnp.concatenate(parts, axis=0)

    for l in range(1, levels + 1):
        dec = jnp.exp2(level_sums(l))
        q_l, k_l = (q * dec).astype(BF16), (k * dec).astype(BF16)
        scores = [jnp.where(lev == l, _dot_nt(q_l[:, cs], k_l[:, cs]), s) for s, cs in zip(scores, head_cols)]
    o = [oh + _dot(s.astype(BF16), v[:, cs]) for oh, s, cs in zip(o, scores, head_cols)]
    o = [oh * lax.rsqrt(jnp.mean(oh * oh, axis=-1, keepdims=True) + NORM_EPS) for oh in o]
    o_ref[...] = (jnp.concatenate(o, axis=1) * gain_ref[...] * _silu(g_ref[...])).astype(o_ref.dtype)


def _hgrn_mixer(z, col0, lb_logits, layer, norm_gain, y, y_col0, *, batch, n_real, seq):
    m = z.shape[0]
    n_lb, width = lb_logits.shape
    c = HGRN_CHUNK
    assert META_BLOCK == c
    hps = HGRN_HEADS_PER_STEP
    wb = hps * HGRN_HEAD_DIM
    n_hg = width // wb
    chunks = seq // c + 1
    sums, lev, levels = _hgrn_tables(c)

    def row_block(b, ci):
        return jnp.where(ci == 0, n_real // c + b, b * (seq // c) + ci - 1)

    def zspec(r):
        cb = (col0 + r * width) // wb
        return pl.BlockSpec((c, wb), lambda b, hg, ci: (row_block(b, ci), cb + hg))

    vec = pl.BlockSpec((1, wb), lambda b, hg, ci: (0, hg))
    return pl.pallas_call(
        functools.partial(_hgrn_kernel, c=c, levels=levels, heads=hps, layer=layer),
        grid=(batch, n_hg, chunks),
        in_specs=[zspec(0), zspec(1), zspec(2), zspec(3),
                  pl.BlockSpec((n_lb, wb), lambda b, hg, ci: (0, hg)), vec,
                  pl.BlockSpec(sums.shape, lambda b, hg, ci: (0, 0)),
                  pl.BlockSpec(lev.shape, lambda b, hg, ci: (0, 0)),
                  pl.BlockSpec(memory_space=pl.ANY)],
        out_specs=pl.BlockSpec((c, wb), lambda b, hg, ci: (row_block(b, ci), y_col0 // wb + hg)),
        out_shape=jax.ShapeDtypeStruct(y.shape, y.dtype),
        scratch_shapes=[pltpu.VMEM((hps, HGRN_HEAD_DIM, HGRN_HEAD_DIM), F32)],
        input_output_aliases={8: 0},
        compiler_params=_params("arbitrary", "arbitrary", "arbitrary"),
    )(z, z, z, z, lb_logits.astype(F32), norm_gain.reshape(1, width),
      jnp.asarray(sums, BF16), jnp.asarray(lev), y)


def _attn_kernel(slope_ref, sink_ref, q_ref, kvc_ref, kvp_ref, kvm_ref, o_ref, *, group, kvs):
    i = pl.program_id(1)
    j = pl.program_id(2)
    hd = ATT_HEAD_DIM
    blk = ATT_BLOCK
    pairs = group // 2
    blocks = kvs * pairs
    low = lax.broadcasted_iota(jnp.int32, (1, 2 * hd), 1) < hd

    def block_diag(kv):
        vk = jnp.concatenate([kv[:, hd:], kv[:, :hd]], axis=1)
        zero = jnp.zeros_like(kv)
        keys = jnp.concatenate([jnp.where(low, kv, zero), jnp.where(low, zero, vk)], axis=0)
        vals = jnp.concatenate([jnp.where(low, vk, zero), jnp.where(low, zero, kv)], axis=0)
        return keys, vals

    def with_ones(vals, n_first):
        shape = (vals.shape[0], 2 * hd)
        first = lax.broadcasted_iota(jnp.int32, shape, 0) < n_first
        ones = jnp.where(first == (lax.broadcasted_iota(jnp.int32, shape, 1) < hd), 1.0, 0.0)
        return jnp.concatenate([vals, ones.astype(BF16)], axis=1)

    pad = jnp.zeros((2 * hd - 2 * N_META, 2 * hd), BF16)
    s, sm, v_band, v_meta = [], [], [], []
    for a in range(kvs):
        cols = slice(a * 2 * hd, (a + 1) * 2 * hd)
        kb, vb = block_diag(jnp.concatenate([kvp_ref[:, cols], kvc_ref[:, cols]], axis=0))
        km, vm = block_diag(kvm_ref[META_BLOCK - N_META:, cols])
        q2 = jnp.concatenate([q_ref[:, (a * pairs + p) * 2 * hd:(a * pairs + p + 1) * 2 * hd]
                              for p in range(pairs)], axis=0)
        q2 = (q2.astype(F32) * hd ** -0.5).astype(BF16)
        s.append(_dot_nt(q2, kb))
        sm.append(_dot_nt(q2, jnp.concatenate([km, pad], axis=0)))
        v_band.append(with_ones(vb, 2 * blk))
        v_meta.append(with_ones(jnp.concatenate([vm, pad], axis=0), N_META))
    s = jnp.concatenate(s, axis=0)
    sm = jnp.concatenate(sm, axis=0)

    r = lax.broadcasted_iota(jnp.int32, (blk, 2 * blk), 0)
    cidx = lax.broadcasted_iota(jnp.int32, (blk, 2 * blk), 1)
    dist = r - cidx + blk
    band_ok = (dist >= 0) & (dist < blk) & ((cidx >= blk) | (i > 0))
    neg_dist = jnp.where(band_ok, -dist.astype(F32), -jnp.inf)
    neg_dist = jnp.concatenate([neg_dist] * blocks, axis=0)

    def per_row(ref, half):
        return jnp.concatenate([jnp.full((blk, 1), ref[j * kvs * group + 2 * p + half], F32)
                                for p in range(blocks)], axis=0)

    mlane = lax.broadcasted_iota(jnp.int32, (1, 2 * hd), 1)
    halves = []
    for half in range(2):
        sink = per_row(sink_ref, half)
        lb = s[:, half * 2 * blk:(half + 1) * 2 * blk] + per_row(slope_ref, half) * neg_dist
        lm = jnp.where((mlane >= half * N_META) & (mlane < (half + 1) * N_META), sm, -jnp.inf)
        mx = jnp.maximum(jnp.maximum(jnp.maximum(lb[:, :blk], lb[:, blk:]), lm).max(-1, keepdims=True), sink)
        halves.append((jnp.exp(lb - mx), jnp.exp(lm - mx), jnp.exp(sink - mx)))
    pb = jnp.concatenate([halves[0][0], halves[1][0]], axis=1).astype(BF16)
    pm = (halves[0][1] + halves[1][1]).astype(BF16)
    rows = [slice(a * pairs * blk, (a + 1) * pairs * blk) for a in range(kvs)]
    acc = jnp.concatenate([_dot(pb[rs], vb) + _dot(pm[rs], vm) for rs, vb, vm in zip(rows, v_band, v_meta)],
                          axis=0)
    o = acc[:, :2 * hd] / (acc[:, 2 * hd:] + jnp.where(low, halves[0][2], halves[1][2]))
    for p in range(blocks):
        o_ref[:, p * 2 * hd:(p + 1) * 2 * hd] = o[p * blk:(p + 1) * blk].astype(o_ref.dtype)


def _attention(z, slopes, sinks, *, batch, n_real, seq):
    hd = ATT_HEAD_DIM
    n_q = slopes.shape[0]
    group = n_q // ATT_KV_HEADS
    kvs = ATT_KV_PER_STEP
    qw = kvs * group * hd
    kvw = kvs * 2 * hd
    kv0 = n_q * hd // kvw
    nblk = seq // ATT_BLOCK
    smem = pl.BlockSpec(memory_space=pltpu.SMEM)
    kv_spec = lambda rows: pl.BlockSpec((ATT_BLOCK, kvw), lambda b, i, j: (rows(b, i), kv0 + j))
    return pl.pallas_call(
        functools.partial(_attn_kernel, group=group, kvs=kvs),
        grid=(batch, nblk, ATT_KV_HEADS // kvs),
        in_specs=[smem, smem,
                  pl.BlockSpec((ATT_BLOCK, qw), lambda b, i, j: (b * nblk + i, j)),
                  kv_spec(lambda b, i: b * nblk + i),
                  kv_spec(lambda b, i: b * nblk + jnp.maximum(i - 1, 0)),
                  kv_spec(lambda b, i: n_real // ATT_BLOCK + b)],
        out_specs=pl.BlockSpec((ATT_BLOCK, qw), lambda b, i, j: (b * nblk + i, j)),
        out_shape=jax.ShapeDtypeStruct((n_real, n_q * hd), BF16),
        compiler_params=_params("arbitrary", "arbitrary", "arbitrary"),
    )(slopes, sinks, z, z, z, z)


def _route_kernel(h_ref, g_ref, r_ref, u_ref, route_ref, count_ref, carry_ref, *, tm):
    @pl.when(pl.program_id(0) == 0)
    def _():
        carry_ref[...] = jnp.zeros_like(carry_ref)

    x = h_ref[...]
    u = x * lax.rsqrt(jnp.mean(x * x, axis=-1, keepdims=True) + NORM_EPS) * g_ref[...]
    bits = lax.bitcast_convert_type(u.astype(BF16).astype(F32), jnp.uint32)
    half = bits.shape[1] // 2
    u_ref[...] = (bits[:, :half] >> 16) | (bits[:, half:] & jnp.uint32(0xFFFF0000))
    u_hi = u.astype(BF16)
    rem = u - u_hi.astype(F32)
    u_mid = rem.astype(BF16)
    u_lo = (rem - u_mid.astype(F32)).astype(BF16)
    router = r_ref[...]
    parts = _dot(u_hi, router) + _dot(u_mid, router) + _dot(u_lo, router)
    logits = (parts + pltpu.roll(parts, LANES - N_EXPERTS, axis=1)
              + pltpu.roll(parts, LANES - 2 * N_EXPERTS, axis=1))
    lane = lax.broadcasted_iota(jnp.int32, (tm, LANES), 1)
    lg = jnp.where(lane < N_EXPERTS, logits, -jnp.inf)
    m1 = lg.max(-1, keepdims=True)
    i1 = jnp.where(lg == m1, lane, LANES).min(-1, keepdims=True)
    lg2 = jnp.where(lane == i1, -jnp.inf, lg)
    m2 = lg2.max(-1, keepdims=True)
    i2 = jnp.where(lg2 == m2, lane, LANES).min(-1, keepdims=True)
    e2 = jnp.exp(m2 - m1)
    g1 = 1.0 / (1.0 + e2)
    g2 = e2 / (1.0 + e2)
    chosen = (lane == i1) | (lane == i2)
    onehot = jnp.where(chosen, 1.0, 0.0)
    rr = lax.broadcasted_iota(jnp.int32, (tm, tm), 0)
    cc = lax.broadcasted_iota(jnp.int32, (tm, tm), 1)
    before = jnp.where(cc < rr, 1.0, 0.0).astype(BF16)
    rank = _dot(before, onehot.astype(BF16)) + carry_ref[...]
    r1 = jnp.where(lane == i1, rank, 0.0).sum(-1, keepdims=True)
    r2 = jnp.where(lane == i2, rank, 0.0).sum(-1, keepdims=True)
    carry_ref[...] = carry_ref[...] + onehot.sum(0, keepdims=True)
    count_ref[...] = carry_ref[...]
    packed = jnp.where(lane == 0, i1.astype(F32), 0.0)
    for idx, val in ((1, i2.astype(F32)), (2, g1), (3, g2), (4, r1), (5, r2)):
        packed = jnp.where(lane == idx, val, packed)
    route_ref[...] = packed


def _route(h, gain, router, *, n_rows, tm=ROW_TILE):
    d = h.shape[1]
    r_hi = router.astype(BF16)
    rem = router.astype(F32) - r_hi.astype(F32)
    r_mid = rem.astype(BF16)
    r_lo = (rem - r_mid.astype(F32)).astype(BF16)
    router_pad = jnp.pad(jnp.concatenate([r_hi, r_mid, r_lo], axis=1), ((0, 0), (0, LANES - 3 * N_EXPERTS)))
    return pl.pallas_call(
        functools.partial(_route_kernel, tm=tm),
        grid=(n_rows // tm,),
        in_specs=[pl.BlockSpec((tm, d), lambda i: (i, 0)),
                  pl.BlockSpec((1, d), lambda i: (0, 0)),
                  pl.BlockSpec((d, LANES), lambda i: (0, 0))],
        out_specs=[pl.BlockSpec((tm, d // 2), lambda i: (i, 0)),
                   pl.BlockSpec((tm, LANES), lambda i: (i, 0)),
                   pl.BlockSpec((1, LANES), lambda i: (0, 0))],
        out_shape=[jax.ShapeDtypeStruct((n_rows, d // 2), jnp.uint32),
                   jax.ShapeDtypeStruct((n_rows, LANES), F32),
                   jax.ShapeDtypeStruct((1, LANES), F32)],
        scratch_shapes=[pltpu.VMEM((1, LANES), F32)],
        compiler_params=_params("arbitrary"),
    )(h, gain.reshape(1, d), router_pad)


def _dispatch_kernel(slot_ref, u_ref, xs_in_ref, xs_ref, sem, *, tt):
    del xs_in_ref

    def copy(t, kk):
        return pltpu.make_async_copy(u_ref.at[pl.ds(t, 1), :],
                                     xs_ref.at[pl.ds(slot_ref[0, 0, TOP_K * t + kk], 1), :], sem)

    def issue(t, carry):
        for kk in range(TOP_K):
            copy(t, kk).start()
        return carry

    def drain(t, carry):
        for kk in range(TOP_K):
            copy(t, kk).wait()
        return carry

    lax.fori_loop(0, tt, issue, 0)
    lax.fori_loop(0, tt, drain, 0)


def _dispatch(u, slots, n_slots, *, tt=ROW_TILE):
    n, d = u.shape
    return pl.pallas_call(
        functools.partial(_dispatch_kernel, tt=tt),
        grid=(n // tt,),
        in_specs=[pl.BlockSpec((1, 1, TOP_K * tt), lambda i: (i, 0, 0), memory_space=pltpu.SMEM),
                  pl.BlockSpec((tt, d), lambda i: (i, 0)),
                  pl.BlockSpec(memory_space=pl.ANY)],
        out_specs=pl.BlockSpec(memory_space=pl.ANY),
        out_shape=jax.ShapeDtypeStruct((n_slots, d), u.dtype),
        scratch_shapes=[pltpu.SemaphoreType.DMA(())],
        input_output_aliases={2: 0},
        compiler_params=_params("arbitrary"),
    )(slots.reshape(n // tt, 1, TOP_K * tt), u, jnp.zeros((n_slots, d), u.dtype))


def _new_expert(be_ref, i):
    return (i == 0) | (be_ref[i] != be_ref[jnp.maximum(i - 1, 0)])


def _by_fill(filled, o_ref, compute):
    tm = o_ref.shape[0]
    half = tm // 2

    @pl.when(filled > half)
    def _():
        o_ref[...] = compute(slice(0, tm)).astype(o_ref.dtype)

    @pl.when((filled > 0) & (filled <= half))
    def _():
        o_ref[:half, :] = compute(slice(0, half)).astype(o_ref.dtype)
        o_ref[half:, :] = jnp.zeros((tm - half, o_ref.shape[1]), o_ref.dtype)

    @pl.when(filled == 0)
    def _():
        o_ref[...] = jnp.zeros_like(o_ref)


def _moe_up_kernel(be_ref, nb_ref, fill_ref, a_ref, wg_ref, wu_ref, o_ref, wgb_ref, wub_ref):
    del nb_ref
    i = pl.program_id(1)

    @pl.when(_new_expert(be_ref, i))
    def _():
        wgb_ref[...] = wg_ref[0].astype(BF16)
        wub_ref[...] = wu_ref[0].astype(BF16)

    def compute(rows):
        packed = a_ref[rows, :]
        half = packed.shape[1]
        a_lo = lax.bitcast_convert_type(packed << 16, F32).astype(BF16)
        a_hi = lax.bitcast_convert_type(packed & jnp.uint32(0xFFFF0000), F32).astype(BF16)
        gate = _dot(a_lo, wgb_ref[:half, :]) + _dot(a_hi, wgb_ref[half:, :])
        up = _dot(a_lo, wub_ref[:half, :]) + _dot(a_hi, wub_ref[half:, :])
        return _silu(gate) * up

    _by_fill(fill_ref[i], o_ref, compute)


def _moe_up(xs, block_expert, n_used, block_fill, wg, wu, *, tn, tm=ROW_TILE):
    ns, kp = xs.shape
    k, n = wg.shape[1:]
    assert k == 2 * kp
    rows = lambda j, i, be, nb, fill: (jnp.minimum(i, nb[0] - 1), 0)
    wspec = pl.BlockSpec((1, k, tn), lambda j, i, be, nb, fill: (be[i], 0, j))
    return pl.pallas_call(
        _moe_up_kernel,
        grid_spec=pltpu.PrefetchScalarGridSpec(
            num_scalar_prefetch=3,
            grid=(n // tn, ns // tm),
            in_specs=[pl.BlockSpec((tm, kp), rows), wspec, wspec],
            out_specs=pl.BlockSpec((tm, tn), lambda j, i, be, nb, fill: (i, j)),
            scratch_shapes=[pltpu.VMEM((k, tn), BF16), pltpu.VMEM((k, tn), BF16)]),
        out_shape=jax.ShapeDtypeStruct((ns, n), BF16),
        compiler_params=_params("arbitrary", "arbitrary"),
    )(block_expert, n_used, block_fill, xs, wg, wu)


def _moe_down_kernel(be_ref, nb_ref, fill_ref, a_ref, w_ref, o_ref, wb_ref):
    del nb_ref
    i = pl.program_id(1)

    @pl.when(_new_expert(be_ref, i))
    def _():
        wb_ref[...] = w_ref[0].astype(BF16)

    _by_fill(fill_ref[i], o_ref, lambda rows: _dot(a_ref[rows, :], wb_ref[...]))


def _moe_down(hmid, block_expert, n_used, block_fill, w, *, tn, tm=ROW_TILE):
    ns, k = hmid.shape
    n = w.shape[2]
    return pl.pallas_call(
        _moe_down_kernel,
        grid_spec=pltpu.PrefetchScalarGridSpec(
            num_scalar_prefetch=3,
            grid=(n // tn, ns // tm),
            in_specs=[pl.BlockSpec((tm, k), lambda j, i, be, nb, fill: (jnp.minimum(i, nb[0] - 1), 0)),
                      pl.BlockSpec((1, k, tn), lambda j, i, be, nb, fill: (be[i], 0, j))],
            out_specs=pl.BlockSpec((tm, tn), lambda j, i, be, nb, fill: (i, j)),
            scratch_shapes=[pltpu.VMEM((k, tn), BF16)]),
        out_shape=jax.ShapeDtypeStruct((ns, n), F32),
        compiler_params=_params("arbitrary", "arbitrary"),
    )(block_expert, n_used, block_fill, hmid, w)


def _combine_kernel(slot_ref, next_slot_ref, h_ref, route_ref, g_ref, y_ref, o_ref, buf_ref, sem, *, tt):
    i = pl.program_id(0)
    cur = i % 2

    def copy(slots, buf, t, kk):
        return pltpu.make_async_copy(y_ref.at[pl.ds(slots[0, 0, TOP_K * t + kk], 1), :],
                                     buf_ref.at[buf, kk, pl.ds(t, 1), :], sem.at[buf])

    def fetch(slots, buf):
        def body(t, carry):
            for kk in range(TOP_K):
                copy(slots, buf, t, kk).start()
            return carry
        lax.fori_loop(0, tt, body, 0)

    @pl.when(i == 0)
    def _():
        fetch(slot_ref, cur)

    @pl.when(i + 1 < pl.num_programs(0))
    def _():
        fetch(next_slot_ref, 1 - cur)

    def drain(t, carry):
        for kk in range(TOP_K):
            copy(slot_ref, cur, t, kk).wait()
        return carry

    lax.fori_loop(0, tt, drain, 0)
    route = route_ref[...]
    x = h_ref[...] + (buf_ref[cur, 0] * route[:, TOP_K:TOP_K + 1] + buf_ref[cur, 1] * route[:, TOP_K + 1:TOP_K + 2])
    y = x * lax.rsqrt(jnp.mean(x * x, axis=-1, keepdims=True) + NORM_EPS)
    o_ref[...] = y * g_ref[...]


def _combine_norm(h, y, slots, route, gain, *, n_rows, tt=256):
    d = h.shape[1]
    steps = n_rows // tt
    slots = slots.reshape(steps, 1, TOP_K * tt)
    return pl.pallas_call(
        functools.partial(_combine_kernel, tt=tt),
        grid=(steps,),
        in_specs=[pl.BlockSpec((1, 1, TOP_K * tt), lambda i: (i, 0, 0), memory_space=pltpu.SMEM),
                  pl.BlockSpec((1, 1, TOP_K * tt), lambda i: (jnp.minimum(i + 1, steps - 1), 0, 0),
                               memory_space=pltpu.SMEM),
                  pl.BlockSpec((tt, d), lambda i: (i, 0)),
                  pl.BlockSpec((tt, LANES), lambda i: (i, 0)),
                  pl.BlockSpec((1, d), lambda i: (0, 0)),
                  pl.BlockSpec(memory_space=pl.ANY)],
        out_specs=pl.BlockSpec((tt, d), lambda i: (i, 0)),
        out_shape=jax.ShapeDtypeStruct((n_rows, d), F32),
        scratch_shapes=[pltpu.VMEM((2, TOP_K, tt, d), F32), pltpu.SemaphoreType.DMA((2,))],
        compiler_params=_params("arbitrary"),
    )(slots, slots, h, route, gain.reshape(1, d), y)


def _even_layer(h, norm_mix, norm_ffn, w_in, w_out, pool_w, pool_scale, lb_logits, layer, hgrn_norm,
                w_gate, w_up, w_down, *, batch, n_real, seq):
    pool_width = pool_scale.shape[0]
    u = _rmsnorm(h, norm_mix)
    z = _matmul(u, w_in.astype(BF16), tn=1024, out_dtype=F32)
    y = _pool_mixer(z, pool_w.astype(BF16), pool_scale, out_width=w_out.shape[0], n_real=n_real, seq=seq)
    y = _hgrn_mixer(z, pool_width, lb_logits, layer, hgrn_norm, y, pool_width, batch=batch, n_real=n_real, seq=seq)
    h = _matmul_residual(y, w_out.astype(BF16), h, tn=1024)
    u = _rmsnorm(h, norm_ffn)
    mid = _matmul_swiglu(u, w_gate, w_up, tn=256, tm=3 * ROW_TILE)
    return _matmul_residual(mid, w_down.astype(BF16), h, tn=512)


def _odd_layer(h, norm_mix, norm_ffn, final_norm, w_in, w_out, sinks, router, w_gate, w_up, w_down,
               *, batch, n_real, seq):
    d = h.shape[1]
    hd = ATT_HEAD_DIM
    n_q = sinks.shape[0]
    nq = n_q * hd
    nkv = ATT_KV_HEADS * hd
    u = _rmsnorm(h, norm_mix)
    w = w_in.astype(BF16)
    wk = w[:, nq:nq + nkv].reshape(d, ATT_KV_HEADS, hd)
    wv = w[:, nq + nkv:].reshape(d, ATT_KV_HEADS, hd)
    w_perm = jnp.concatenate([w[:, :nq], jnp.concatenate([wk, wv], axis=2).reshape(d, 2 * nkv)], axis=1)
    z = _matmul(u, w_perm, tn=1024, out_dtype=BF16)
    slopes = 2.0 ** (-8.0 * jnp.arange(1, n_q + 1, dtype=F32) / n_q)
    att = _attention(z, slopes, sinks.astype(F32), batch=batch, n_real=n_real, seq=seq)
    h = _matmul_residual(att, w_out.astype(BF16), h, tn=1024, m_rows=n_real)
    return _moe_ffn_norm(h, norm_ffn, final_norm, router, w_gate, w_up, w_down, n_real=n_real)


def _moe_ffn_norm(h, norm_ffn, final_norm, router, w_gate, w_up, w_down, *, n_real):
    u, route, counts = _route(h, norm_ffn, router, n_rows=n_real)
    blk = ROW_TILE
    n_blocks = n_real * TOP_K // blk + N_EXPERTS
    e_idx = route[:, 0:TOP_K].astype(jnp.int32)
    rank = route[:, 2 * TOP_K:3 * TOP_K].astype(jnp.int32)
    cnt = counts[0, :N_EXPERTS].astype(jnp.int32)
    padded = (cnt + blk - 1) // blk * blk
    pad_end = jnp.cumsum(padded)
    pad_start = pad_end - padded
    slots = pad_start[e_idx] + rank
    n_used = (pad_end[-1] // blk).astype(jnp.int32).reshape(1)
    blocks = jnp.minimum(jnp.arange(n_blocks, dtype=jnp.int32), n_used[0] - 1)
    block_expert = jnp.minimum(jnp.searchsorted(pad_end, blocks * blk, side='right'),
                               N_EXPERTS - 1).astype(jnp.int32)
    all_blocks = jnp.arange(n_blocks, dtype=jnp.int32)
    block_fill = jnp.clip((pad_start + cnt)[block_expert] - all_blocks * blk, 0, blk)
    block_fill = jnp.where(all_blocks < n_used[0], block_fill, 0).astype(jnp.int32)
    xs = _dispatch(u, slots, n_blocks * blk)
    mid = _moe_up(xs, block_expert, n_used, block_fill, w_gate, w_up, tn=512)
    y = _moe_down(mid, block_expert, n_used, block_fill, w_down, tn=min(512, h.shape[1]))
    return _combine_norm(h, y, slots, route, final_norm, n_rows=n_real)


def kernel(x, meta_tokens, norm_mix, norm_ffn, final_norm, even_w_in, even_w_out, pool_w_group, pool_scale, hgrn_lb_logits, hgrn_norm, odd_w_in, odd_w_out, attn_sinks, ffn_w_gate, ffn_w_up, ffn_w_down, moe_router, moe_w_gate, moe_w_up, moe_w_down):
    batch, seq, d = x.shape
    n_real = batch * seq
    assert seq % ROW_TILE == 0 and (batch * META_BLOCK) % ROW_TILE == 0
    assert norm_mix.shape[0] == 2, "one even and one odd layer"
    meta_block = jnp.concatenate([jnp.zeros((META_BLOCK - N_META, d), F32), meta_tokens.astype(F32)], axis=0)
    h = (x.reshape(n_real, d), jnp.tile(meta_block, (batch, 1)))
    dims = dict(batch=batch, n_real=n_real, seq=seq)
    h = _even_layer(h, norm_mix[0], norm_ffn[0], even_w_in[0], even_w_out[0], pool_w_group[0],
                    pool_scale[0], hgrn_lb_logits, 0, hgrn_norm[0], ffn_w_gate[0], ffn_w_up[0],
                    ffn_w_down[0], **dims)
    out = _odd_layer(h, norm_mix[1], norm_ffn[1], final_norm, odd_w_in[0], odd_w_out[0], attn_sinks[0],
                     moe_router[0], moe_w_gate[0], moe_w_up[0], moe_w_down[0], **dims)
    return out.reshape(batch, seq, d)
```

```python
import functools
import math

import numpy as np
import jax
import jax.numpy as jnp
from jax import lax
from jax.experimental import pallas as pl
from jax.experimental.pallas import tpu as pltpu

F32 = jnp.float32
BF16 = jnp.bfloat16

N_META = 16
NORM_EPS = 1e-5
POOL_WINDOWS = (2, 4, 8, 16)
POOL_HALO = 16
HGRN_HEAD_DIM = 128
HGRN_CHUNK = 128
HGRN_HEADS_PER_STEP = 16
HGRN_TABLE_LEVELS = 3
ATT_HEAD_DIM = 64
ATT_KV_HEADS = 8
ATT_BLOCK = 128
ATT_KV_PER_STEP = 8
N_EXPERTS = 8
TOP_K = 2

META_BLOCK = 128
ROW_TILE = 512
LANES = 128
VMEM_LIMIT = 56 * 1024 * 1024


def _params(*sem):
    return pltpu.CompilerParams(dimension_semantics=sem, vmem_limit_bytes=VMEM_LIMIT)


def _silu(x):
    return x * jax.nn.sigmoid(x)


def _dot(a, b):
    return jnp.dot(a, b, preferred_element_type=F32)


def _dot_nt(a, b):
    return lax.dot_general(a, b, (((1,), (1,)), ((), ())), preferred_element_type=F32)


def _dot_tn(a, b):
    return lax.dot_general(a, b, (((0,), (0,)), ((), ())), preferred_element_type=F32)


def _row_source(h, tm, width, col):
    if not isinstance(h, tuple):
        return [h], [pl.BlockSpec((tm, width), lambda *ids: (ids[0], col(*ids)))], None
    head, tail = h
    assert tail.shape[0] == tm and head.shape[0] % tm == 0
    head_tiles = head.shape[0] // tm
    specs = [pl.BlockSpec((tm, width), lambda *ids: (jnp.minimum(ids[0], head_tiles - 1), col(*ids))),
             pl.BlockSpec((tm, width), lambda *ids: (0, col(*ids)))]
    return [head, tail], specs, head_tiles


def _read_rows(refs, head_tiles):
    if head_tiles is None:
        return refs[0][...]
    return jnp.where(pl.program_id(0) < head_tiles, refs[0][...], refs[1][...])


def _rmsnorm_kernel(*refs, head_tiles):
    *h_refs, g_ref, o_ref = refs
    x = _read_rows(h_refs, head_tiles)
    y = x * lax.rsqrt(jnp.mean(x * x, axis=-1, keepdims=True) + NORM_EPS)
    o_ref[...] = (y * g_ref[...]).astype(o_ref.dtype)


def _rmsnorm(h, gain, out_dtype=BF16, tm=ROW_TILE):
    d = gain.shape[0]
    arrays, specs, head_tiles = _row_source(h, tm, d, lambda *ids: 0)
    m = sum(a.shape[0] for a in arrays)
    return pl.pallas_call(
        functools.partial(_rmsnorm_kernel, head_tiles=head_tiles),
        grid=(m // tm,),
        in_specs=[*specs, pl.BlockSpec((1, d), lambda i: (0, 0))],
        out_specs=pl.BlockSpec((tm, d), lambda i: (i, 0)),
        out_shape=jax.ShapeDtypeStruct((m, d), out_dtype),
        compiler_params=_params("arbitrary"),
    )(*arrays, gain.reshape(1, d))


def _mm_kernel(a_ref, w_ref, o_ref):
    o_ref[...] = _dot(a_ref[...], w_ref[...]).astype(o_ref.dtype)


def _mm_res_kernel(a_ref, w_ref, *refs, head_tiles):
    *r_refs, o_ref = refs
    o_ref[...] = _read_rows(r_refs, head_tiles) + _dot(a_ref[...], w_ref[...])


def _mm_swiglu_kernel(a_ref, wg_ref, wu_ref, o_ref, wgb_ref, wub_ref):
    @pl.when(pl.program_id(1) == 0)
    def _():
        wgb_ref[...] = wg_ref[...].astype(BF16)
        wub_ref[...] = wu_ref[...].astype(BF16)

    a = a_ref[...]
    o_ref[...] = (_silu(_dot(a, wgb_ref[...])) * _dot(a, wub_ref[...])).astype(o_ref.dtype)


def _matmul(a, w, *, tn, out_dtype, tm=ROW_TILE):
    m, k = a.shape
    n = w.shape[1]
    return pl.pallas_call(
        _mm_kernel,
        grid=(n // tn, m // tm),
        in_specs=[pl.BlockSpec((tm, k), lambda j, i: (i, 0)), pl.BlockSpec((k, tn), lambda j, i: (0, j))],
        out_specs=pl.BlockSpec((tm, tn), lambda j, i: (i, j)),
        out_shape=jax.ShapeDtypeStruct((m, n), out_dtype),
        compiler_params=_params("arbitrary", "arbitrary"),
    )(a, w)


def _matmul_residual(a, w, res, *, tn, tm=ROW_TILE, m_rows=None):
    m = a.shape[0] if m_rows is None else m_rows
    k = a.shape[1]
    n = w.shape[1]
    res_arrays, res_specs, head_tiles = _row_source(res, tm, tn, lambda i, j: j)
    return pl.pallas_call(
        functools.partial(_mm_res_kernel, head_tiles=head_tiles),
        grid=(m // tm, n // tn),
        in_specs=[pl.BlockSpec((tm, k), lambda i, j: (i, 0)),
                  pl.BlockSpec((k, tn), lambda i, j: (0, j)),
                  *res_specs],
        out_specs=pl.BlockSpec((tm, tn), lambda i, j: (i, j)),
        out_shape=jax.ShapeDtypeStruct((m, n), F32),
        compiler_params=_params("arbitrary", "arbitrary"),
    )(a, w, *res_arrays)


def _matmul_swiglu(a, wg, wu, *, tn, tm=ROW_TILE):
    m, k = a.shape
    n = wg.shape[1]
    return pl.pallas_call(
        _mm_swiglu_kernel,
        grid=(n // tn, m // tm),
        in_specs=[pl.BlockSpec((tm, k), lambda j, i: (i, 0)),
                  pl.BlockSpec((k, tn), lambda j, i: (0, j)),
                  pl.BlockSpec((k, tn), lambda j, i: (0, j))],
        out_specs=pl.BlockSpec((tm, tn), lambda j, i: (i, j)),
        out_shape=jax.ShapeDtypeStruct((m, n), BF16),
        scratch_shapes=[pltpu.VMEM((k, tn), BF16), pltpu.VMEM((k, tn), BF16)],
        compiler_params=_params("arbitrary", "arbitrary"),
    )(a, wg, wu)


def _pool_kernel(z_ref, halo_ref, wg_ref, sc_ref, o_ref, *, tm, n_real, group):
    i = pl.program_id(0)
    rows = i * tm + lax.broadcasted_iota(jnp.int32, (tm, 1), 0)
    in_meta = rows >= n_real
    pos = (rows % META_BLOCK) - (META_BLOCK - N_META)
    is_pad = in_meta & (pos < 0)
    for gi, w in enumerate(POOL_WINDOWS):
        cols = slice(gi * group, (gi + 1) * group)
        x = z_ref[:, cols]
        s = jnp.concatenate([halo_ref[:, cols], x], axis=0)
        step = 1
        while step < w:
            s = s + pltpu.roll(s, step, axis=0)
            step *= 2
        count = jnp.where(in_meta, jnp.clip(pos + 1, 1, w), w).astype(F32)
        d = s[POOL_HALO:] / count - x
        y = _dot(d.astype(BF16), wg_ref[gi]) * sc_ref[:, cols]
        o_ref[:, cols] = jnp.where(is_pad, 0.0, y).astype(o_ref.dtype)
    rest = o_ref.shape[1] - len(POOL_WINDOWS) * group
    o_ref[:, len(POOL_WINDOWS) * group:] = jnp.zeros((tm, rest), o_ref.dtype)


def _pool_mixer(z, w_group, scale, *, out_width, n_real, seq, tm=ROW_TILE):
    m = z.shape[0]
    n_g, group, _ = w_group.shape
    width = n_g * group
    tiles_per_seq = seq // tm
    hb = tm // POOL_HALO

    def halo_index(i):
        b = i // tiles_per_seq
        meta_tail = (n_real + b * META_BLOCK + META_BLOCK - POOL_HALO) // POOL_HALO
        real = jnp.where(i % tiles_per_seq == 0, meta_tail, i * hb - 1)
        return (jnp.where(i * tm >= n_real, n_real // POOL_HALO, real), 0)

    return pl.pallas_call(
        functools.partial(_pool_kernel, tm=tm, n_real=n_real, group=group),
        grid=(m // tm,),
        in_specs=[pl.BlockSpec((tm, width), lambda i: (i, 0)),
                  pl.BlockSpec((POOL_HALO, width), halo_index),
                  pl.BlockSpec((n_g, group, group), lambda i: (0, 0, 0)),
                  pl.BlockSpec((1, width), lambda i: (0, 0))],
        out_specs=pl.BlockSpec((tm, out_width), lambda i: (i, 0)),
        out_shape=jax.ShapeDtypeStruct((m, out_width), BF16),
        compiler_params=_params("arbitrary"),
    )(z, z, w_group, scale.reshape(1, width))


def _hgrn_tables(c):
    levels = int(math.log2(c))
    t = np.arange(c)
    blocks = [(t[None, :] <= t[:, None])]
    for l in range(1, HGRN_TABLE_LEVELS + 1):
        bs, hs = 1 << l, 1 << (l - 1)
        mid = (t // bs) * bs + hs - 1
        upper = (t % bs) >= hs
        eq = upper[:, None] & (t[None, :] > mid[:, None]) & (t[None, :] <= t[:, None])
        ek = (~upper)[:, None] & (t[None, :] > t[:, None]) & (t[None, :] <= mid[:, None])
        blocks.append(eq | ek)
    sums = np.concatenate(blocks, axis=0).astype(np.float32)
    sums = np.concatenate([sums, sums], axis=1)
    x = t[:, None] ^ t[None, :]
    lev = np.where(t[None, :] > t[:, None], -1,
                   np.floor(np.log2(np.maximum(x, 1))).astype(np.int32) + (x > 0))
    return sums, lev.astype(np.int32), levels


def _hgrn_kernel(q_ref, f_ref, i_ref, g_ref, lbl_ref, gain_ref, sums_ref, lev_ref, y_in_ref, o_ref, state_ref,
                 *, c, levels, heads, layer):
    del y_in_ref
    @pl.when(pl.program_id(2) == 0)
    def _():
        state_ref[...] = jnp.zeros_like(state_ref)

    lev = lev_ref[...]
    hd = HGRN_HEAD_DIM
    head_cols = [slice(h * hd, (h + 1) * hd) for h in range(heads)]
    lbl = lbl_ref[...]
    ex = jnp.exp(lbl - lbl.max(0, keepdims=True))
    lb = ex[0:layer + 1].sum(0, keepdims=True) / ex.sum(0, keepdims=True)
    fg = lb + (1.0 - lb) * jax.nn.sigmoid(f_ref[...])
    log2_f = jnp.log2(fg)
    k = 1.0 - fg
    q = _silu(q_ref[...])
    v = i_ref[...].astype(BF16)
    hi = log2_f.astype(BF16)
    r1 = log2_f - hi.astype(F32)
    mid = r1.astype(BF16)
    lo = (r1 - mid.astype(F32)).astype(BF16)
    e = (_dot(sums_ref[...], jnp.concatenate([hi, mid], axis=0))
         + _dot(sums_ref[:, 0:c], lo))
    cum = e[0:c]
    total = cum[c - 1:c, :]
    q_dec = (q * jnp.exp2(cum)).astype(BF16)
    k_dec = (k * jnp.exp2(total - cum)).astype(BF16)
    carry = jnp.exp2(total)
    states = [state_ref[h] for h in range(heads)]
    o = [_dot_nt(q_dec[:, cs], st.astype(BF16)) for cs, st in zip(head_cols, states)]
    for h, (cs, st) in enumerate(zip(head_cols, states)):
        state_ref[h] = st * carry[:, cs] + _dot_tn(v[:, cs], k_dec[:, cs])
    q_l, k_l = q.astype(BF16), k.astype(BF16)
    scores = [jnp.where(lev == 0, _dot_nt(q_l[:, cs], k_l[:, cs]), 0.0) for cs in head_cols]
    def level_sums(l):
        if l <= HGRN_TABLE_LEVELS:
            return e[l * c:(l + 1) * c]
        bs, hs = 1 << l, 1 << (l - 1)
        parts = []
        for start in range(0, c, bs):
            middle = cum[start + hs - 1:start + hs, :]
            parts += [middle - cum[start:start + hs], cum[start + hs:start + bs] - middle]
        return jnp.concatenate(parts, axis=0)

    for l in range(1, levels + 1):
        dec = jnp.exp2(level_sums(l))
        q_l, k_l = (q * dec).astype(BF16), (k * dec).astype(BF16)
        scores = [jnp.where(lev == l, _dot_nt(q_l[:, cs], k_l[:, cs]), s) for s, cs in zip(scores, head_cols)]
    o = [oh + _dot(s.astype(BF16), v[:, cs]) for oh, s, cs in zip(o, scores, head_cols)]
    o = [oh * lax.rsqrt(jnp.mean(oh * oh, axis=-1, keepdims=True) + NORM_EPS) for oh in o]
    o_ref[...] = (jnp.concatenate(o, axis=1) * gain_ref[...] * _silu(g_ref[...])).astype(o_ref.dtype)


def _hgrn_mixer(z, col0, lb_logits, layer, norm_gain, y, y_col0, *, batch, n_real, seq):
    m = z.shape[0]
    n_lb, width = lb_logits.shape
    c = HGRN_CHUNK
    assert META_BLOCK == c
    hps = HGRN_HEADS_PER_STEP
    wb = hps * HGRN_HEAD_DIM
    n_hg = width // wb
    chunks = seq // c + 1
    sums, lev, levels = _hgrn_tables(c)

    def row_block(b, ci):
        return jnp.where(ci == 0, n_real // c + b, b * (seq // c) + ci - 1)

    def zspec(r):
        cb = (col0 + r * width) // wb
        return pl.BlockSpec((c, wb), lambda b, hg, ci: (row_block(b, ci), cb + hg))

    vec = pl.BlockSpec((1, wb), lambda b, hg, ci: (0, hg))
    return pl.pallas_call(
        functools.partial(_hgrn_kernel, c=c, levels=levels, heads=hps, layer=layer),
        grid=(batch, n_hg, chunks),
        in_specs=[zspec(0), zspec(1), zspec(2), zspec(3),
                  pl.BlockSpec((n_lb, wb), lambda b, hg, ci: (0, hg)), vec,
                  pl.BlockSpec(sums.shape, lambda b, hg, ci: (0, 0)),
                  pl.BlockSpec(lev.shape, lambda b, hg, ci: (0, 0)),
                  pl.BlockSpec(memory_space=pl.ANY)],
        out_specs=pl.BlockSpec((c, wb), lambda b, hg, ci: (row_block(b, ci), y_col0 // wb + hg)),
        out_shape=jax.ShapeDtypeStruct(y.shape, y.dtype),
        scratch_shapes=[pltpu.VMEM((hps, HGRN_HEAD_DIM, HGRN_HEAD_DIM), F32)],
        input_output_aliases={8: 0},
        compiler_params=_params("arbitrary", "arbitrary", "arbitrary"),
    )(z, z, z, z, lb_logits.astype(F32), norm_gain.reshape(1, width),
      jnp.asarray(sums, BF16), jnp.asarray(lev), y)


def _attn_kernel(slope_ref, sink_ref, q_ref, kvc_ref, kvp_ref, kvm_ref, o_ref, *, group, kvs):
    i = pl.program_id(1)
    j = pl.program_id(2)
    hd = ATT_HEAD_DIM
    blk = ATT_BLOCK
    pairs = group // 2
    blocks = kvs * pairs
    low = lax.broadcasted_iota(jnp.int32, (1, 2 * hd), 1) < hd

    def block_diag(kv):
        vk = jnp.concatenate([kv[:, hd:], kv[:, :hd]], axis=1)
        zero = jnp.zeros_like(kv)
        keys = jnp.concatenate([jnp.where(low, kv, zero), jnp.where(low, zero, vk)], axis=0)
        vals = jnp.concatenate([jnp.where(low, vk, zero), jnp.where(low, zero, kv)], axis=0)
        return keys, vals

    def with_ones(vals, n_first):
        shape = (vals.shape[0], 2 * hd)
        first = lax.broadcasted_iota(jnp.int32, shape, 0) < n_first
        ones = jnp.where(first == (lax.broadcasted_iota(jnp.int32, shape, 1) < hd), 1.0, 0.0)
        return jnp.concatenate([vals, ones.astype(BF16)], axis=1)

    pad = jnp.zeros((2 * hd - 2 * N_META, 2 * hd), BF16)
    s, sm, v_band, v_meta = [], [], [], []
    for a in range(kvs):
        cols = slice(a * 2 * hd, (a + 1) * 2 * hd)
        kb, vb = block_diag(jnp.concatenate([kvp_ref[:, cols], kvc_ref[:, cols]], axis=0))
        km, vm = block_diag(kvm_ref[META_BLOCK - N_META:, cols])
        q2 = jnp.concatenate([q_ref[:, (a * pairs + p) * 2 * hd:(a * pairs + p + 1) * 2 * hd]
                              for p in range(pairs)], axis=0)
        q2 = (q2.astype(F32) * hd ** -0.5).astype(BF16)
        s.append(_dot_nt(q2, kb))
        sm.append(_dot_nt(q2, jnp.concatenate([km, pad], axis=0)))
        v_band.append(with_ones(vb, 2 * blk))
        v_meta.append(with_ones(jnp.concatenate([vm, pad], axis=0), N_META))
    s = jnp.concatenate(s, axis=0)
    sm = jnp.concatenate(sm, axis=0)

    r = lax.broadcasted_iota(jnp.int32, (blk, 2 * blk), 0)
    cidx = lax.broadcasted_iota(jnp.int32, (blk, 2 * blk), 1)
    dist = r - cidx + blk
    band_ok = (dist >= 0) & (dist < blk) & ((cidx >= blk) | (i > 0))
    neg_dist = jnp.where(band_ok, -dist.astype(F32), -jnp.inf)
    neg_dist = jnp.concatenate([neg_dist] * blocks, axis=0)

    def per_row(ref, half):
        return jnp.concatenate([jnp.full((blk, 1), ref[j * kvs * group + 2 * p + half], F32)
                                for p in range(blocks)], axis=0)

    mlane = lax.broadcasted_iota(jnp.int32, (1, 2 * hd), 1)
    halves = []
    for half in range(2):
        sink = per_row(sink_ref, half)
        lb = s[:, half * 2 * blk:(half + 1) * 2 * blk] + per_row(slope_ref, half) * neg_dist
        lm = jnp.where((mlane >= half * N_META) & (mlane < (half + 1) * N_META), sm, -jnp.inf)
        mx = jnp.maximum(jnp.maximum(jnp.maximum(lb[:, :blk], lb[:, blk:]), lm).max(-1, keepdims=True), sink)
        halves.append((jnp.exp(lb - mx), jnp.exp(lm - mx), jnp.exp(sink - mx)))
    pb = jnp.concatenate([halves[0][0], halves[1][0]], axis=1).astype(BF16)
    pm = (halves[0][1] + halves[1][1]).astype(BF16)
    rows = [slice(a * pairs * blk, (a + 1) * pairs * blk) for a in range(kvs)]
    acc = jnp.concatenate([_dot(pb[rs], vb) + _dot(pm[rs], vm) for rs, vb, vm in zip(rows, v_band, v_meta)],
                          axis=0)
    o = acc[:, :2 * hd] / (acc[:, 2 * hd:] + jnp.where(low, halves[0][2], halves[1][2]))
    for p in range(blocks):
        o_ref[:, p * 2 * hd:(p + 1) * 2 * hd] = o[p * blk:(p + 1) * blk].astype(o_ref.dtype)


def _attention(z, slopes, sinks, *, batch, n_real, seq):
    hd = ATT_HEAD_DIM
    n_q = slopes.shape[0]
    group = n_q // ATT_KV_HEADS
    kvs = ATT_KV_PER_STEP
    qw = kvs * group * hd
    kvw = kvs * 2 * hd
    kv0 = n_q * hd // kvw
    nblk = seq // ATT_BLOCK
    smem = pl.BlockSpec(memory_space=pltpu.SMEM)
    kv_spec = lambda rows: pl.BlockSpec((ATT_BLOCK, kvw), lambda b, i, j: (rows(b, i), kv0 + j))
    return pl.pallas_call(
        functools.partial(_attn_kernel, group=group, kvs=kvs),
        grid=(batch, nblk, ATT_KV_HEADS // kvs),
        in_specs=[smem, smem,
                  pl.BlockSpec((ATT_BLOCK, qw), lambda b, i, j: (b * nblk + i, j)),
                  kv_spec(lambda b, i: b * nblk + i),
                  kv_spec(lambda b, i: b * nblk + jnp.maximum(i - 1, 0)),
                  kv_spec(lambda b, i: n_real // ATT_BLOCK + b)],
        out_specs=pl.BlockSpec((ATT_BLOCK, qw), lambda b, i, j: (b * nblk + i, j)),
        out_shape=jax.ShapeDtypeStruct((n_real, n_q * hd), BF16),
        compiler_params=_params("arbitrary", "arbitrary", "arbitrary"),
    )(slopes, sinks, z, z, z, z)


def _route_kernel(h_ref, g_ref, r_ref, u_ref, route_ref, count_ref, carry_ref, *, tm):
    @pl.when(pl.program_id(0) == 0)
    def _():
        carry_ref[...] = jnp.zeros_like(carry_ref)

    x = h_ref[...]
    u = x * lax.rsqrt(jnp.mean(x * x, axis=-1, keepdims=True) + NORM_EPS) * g_ref[...]
    bits = lax.bitcast_convert_type(u.astype(BF16).astype(F32), jnp.uint32)
    half = bits.shape[1] // 2
    u_ref[...] = (bits[:, :half] >> 16) | (bits[:, half:] & jnp.uint32(0xFFFF0000))
    u_hi = u.astype(BF16)
    rem = u - u_hi.astype(F32)
    u_mid = rem.astype(BF16)
    u_lo = (rem - u_mid.astype(F32)).astype(BF16)
    router = r_ref[...]
    parts = _dot(u_hi, router) + _dot(u_mid, router) + _dot(u_lo, router)
    logits = (parts + pltpu.roll(parts, LANES - N_EXPERTS, axis=1)
              + pltpu.roll(parts, LANES - 2 * N_EXPERTS, axis=1))
    lane = lax.broadcasted_iota(jnp.int32, (tm, LANES), 1)
    lg = jnp.where(lane < N_EXPERTS, logits, -jnp.inf)
    m1 = lg.max(-1, keepdims=True)
    i1 = jnp.where(lg == m1, lane, LANES).min(-1, keepdims=True)
    lg2 = jnp.where(lane == i1, -jnp.inf, lg)
    m2 = lg2.max(-1, keepdims=True)
    i2 = jnp.where(lg2 == m2, lane, LANES).min(-1, keepdims=True)
    e2 = jnp.exp(m2 - m1)
    g1 = 1.0 / (1.0 + e2)
    g2 = e2 / (1.0 + e2)
    chosen = (lane == i1) | (lane == i2)
    onehot = jnp.where(chosen, 1.0, 0.0)
    rr = lax.broadcasted_iota(jnp.int32, (tm, tm), 0)
    cc = lax.broadcasted_iota(jnp.int32, (tm, tm), 1)
    before = jnp.where(cc < rr, 1.0, 0.0).astype(BF16)
    rank = _dot(before, onehot.astype(BF16)) + carry_ref[...]
    r1 = jnp.where(lane == i1, rank, 0.0).sum(-1, keepdims=True)
    r2 = jnp.where(lane == i2, rank, 0.0).sum(-1, keepdims=True)
    carry_ref[...] = carry_ref[...] + onehot.sum(0, keepdims=True)
    count_ref[...] = carry_ref[...]
    packed = jnp.where(lane == 0, i1.astype(F32), 0.0)
    for idx, val in ((1, i2.astype(F32)), (2, g1), (3, g2), (4, r1), (5, r2)):
        packed = jnp.where(lane == idx, val, packed)
    route_ref[...] = packed


def _route(h, gain, router, *, n_rows, tm=ROW_TILE):
    d = h.shape[1]
    r_hi = router.astype(BF16)
    rem = router.astype(F32) - r_hi.astype(F32)
    r_mid = rem.astype(BF16)
    r_lo = (rem - r_mid.astype(F32)).astype(BF16)
    router_pad = jnp.pad(jnp.concatenate([r_hi, r_mid, r_lo], axis=1), ((0, 0), (0, LANES - 3 * N_EXPERTS)))
    return pl.pallas_call(
        functools.partial(_route_kernel, tm=tm),
        grid=(n_rows // tm,),
        in_specs=[pl.BlockSpec((tm, d), lambda i: (i, 0)),
                  pl.BlockSpec((1, d), lambda i: (0, 0)),
                  pl.BlockSpec((d, LANES), lambda i: (0, 0))],
        out_specs=[pl.BlockSpec((tm, d // 2), lambda i: (i, 0)),
                   pl.BlockSpec((tm, LANES), lambda i: (i, 0)),
                   pl.BlockSpec((1, LANES), lambda i: (0, 0))],
        out_shape=[jax.ShapeDtypeStruct((n_rows, d // 2), jnp.uint32),
                   jax.ShapeDtypeStruct((n_rows, LANES), F32),
                   jax.ShapeDtypeStruct((1, LANES), F32)],
        scratch_shapes=[pltpu.VMEM((1, LANES), F32)],
        compiler_params=_params("arbitrary"),
    )(h, gain.reshape(1, d), router_pad)


def _dispatch_kernel(slot_ref, u_ref, xs_in_ref, xs_ref, sem, *, tt):
    del xs_in_ref

    def copy(t, kk):
        return pltpu.make_async_copy(u_ref.at[pl.ds(t, 1), :],
                                     xs_ref.at[pl.ds(slot_ref[0, 0, TOP_K * t + kk], 1), :], sem)

    def issue(t, carry):
        for kk in range(TOP_K):
            copy(t, kk).start()
        return carry

    lax.fori_loop(0, tt, issue, 0)
    for _ in range(TOP_K):
        pltpu.make_async_copy(u_ref, xs_ref.at[pl.ds(0, tt), :], sem).wait()


def _dispatch(u, slots, n_slots, *, tt=ROW_TILE):
    n, d = u.shape
    return pl.pallas_call(
        functools.partial(_dispatch_kernel, tt=tt),
        grid=(n // tt,),
        in_specs=[pl.BlockSpec((1, 1, TOP_K * tt), lambda i: (i, 0, 0), memory_space=pltpu.SMEM),
                  pl.BlockSpec((tt, d), lambda i: (i, 0)),
                  pl.BlockSpec(memory_space=pl.ANY)],
        out_specs=pl.BlockSpec(memory_space=pl.ANY),
        out_shape=jax.ShapeDtypeStruct((n_slots, d), u.dtype),
        scratch_shapes=[pltpu.SemaphoreType.DMA(())],
        input_output_aliases={2: 0},
        compiler_params=_params("arbitrary"),
    )(slots.reshape(n // tt, 1, TOP_K * tt), u, jnp.zeros((n_slots, d), u.dtype))


def _new_expert(be_ref, i):
    return (i == 0) | (be_ref[i] != be_ref[jnp.maximum(i - 1, 0)])


def _moe_up_kernel(be_ref, nb_ref, a_ref, wg_ref, wu_ref, o_ref, wgb_ref, wub_ref):
    i = pl.program_id(1)

    @pl.when(_new_expert(be_ref, i))
    def _():
        wgb_ref[...] = wg_ref[0].astype(BF16)
        wub_ref[...] = wu_ref[0].astype(BF16)

    @pl.when(i < nb_ref[0])
    def _():
        packed = a_ref[...]
        half = packed.shape[1]
        a_lo = lax.bitcast_convert_type(packed << 16, F32).astype(BF16)
        a_hi = lax.bitcast_convert_type(packed & jnp.uint32(0xFFFF0000), F32).astype(BF16)
        gate = _dot(a_lo, wgb_ref[:half, :]) + _dot(a_hi, wgb_ref[half:, :])
        up = _dot(a_lo, wub_ref[:half, :]) + _dot(a_hi, wub_ref[half:, :])
        o_ref[...] = (_silu(gate) * up).astype(o_ref.dtype)

    @pl.when(i >= nb_ref[0])
    def _():
        o_ref[...] = jnp.zeros_like(o_ref)


def _moe_up(xs, block_expert, n_used, wg, wu, *, tn, tm=ROW_TILE):
    ns, kp = xs.shape
    k, n = wg.shape[1:]
    assert k == 2 * kp
    rows = lambda j, i, be, nb: (jnp.minimum(i, nb[0] - 1), 0)
    wspec = pl.BlockSpec((1, k, tn), lambda j, i, be, nb: (be[i], 0, j))
    return pl.pallas_call(
        _moe_up_kernel,
        grid_spec=pltpu.PrefetchScalarGridSpec(
            num_scalar_prefetch=2,
            grid=(n // tn, ns // tm),
            in_specs=[pl.BlockSpec((tm, kp), rows), wspec, wspec],
            out_specs=pl.BlockSpec((tm, tn), lambda j, i, be, nb: (i, j)),
            scratch_shapes=[pltpu.VMEM((k, tn), BF16), pltpu.VMEM((k, tn), BF16)]),
        out_shape=jax.ShapeDtypeStruct((ns, n), BF16),
        compiler_params=_params("arbitrary", "arbitrary"),
    )(block_expert, n_used, xs, wg, wu)


def _moe_down_kernel(be_ref, nb_ref, a_ref, w_ref, o_ref, wb_ref):
    i = pl.program_id(1)

    @pl.when(_new_expert(be_ref, i))
    def _():
        wb_ref[...] = w_ref[0].astype(BF16)

    @pl.when(i < nb_ref[0])
    def _():
        o_ref[...] = _dot(a_ref[...], wb_ref[...])

    @pl.when(i >= nb_ref[0])
    def _():
        o_ref[...] = jnp.zeros_like(o_ref)


def _moe_down(hmid, block_expert, n_used, w, *, tn, tm=ROW_TILE):
    ns, k = hmid.shape
    n = w.shape[2]
    return pl.pallas_call(
        _moe_down_kernel,
        grid_spec=pltpu.PrefetchScalarGridSpec(
            num_scalar_prefetch=2,
            grid=(n // tn, ns // tm),
            in_specs=[pl.BlockSpec((tm, k), lambda j, i, be, nb: (jnp.minimum(i, nb[0] - 1), 0)),
                      pl.BlockSpec((1, k, tn), lambda j, i, be, nb: (be[i], 0, j))],
            out_specs=pl.BlockSpec((tm, tn), lambda j, i, be, nb: (i, j)),
            scratch_shapes=[pltpu.VMEM((k, tn), BF16)]),
        out_shape=jax.ShapeDtypeStruct((ns, n), F32),
        compiler_params=_params("arbitrary", "arbitrary"),
    )(block_expert, n_used, hmid, w)


def _combine_kernel(slot_ref, next_slot_ref, h_ref, route_ref, g_ref, y_ref, o_ref, buf_ref, sem, *, tt):
    i = pl.program_id(0)
    cur = i % 2

    def copy(slots, buf, t, kk):
        return pltpu.make_async_copy(y_ref.at[pl.ds(slots[0, 0, TOP_K * t + kk], 1), :],
                                     buf_ref.at[buf, kk, pl.ds(t, 1), :], sem.at[buf])

    def fetch(slots, buf):
        def body(t, carry):
            for kk in range(TOP_K):
                copy(slots, buf, t, kk).start()
            return carry
        lax.fori_loop(0, tt, body, 0)

    @pl.when(i == 0)
    def _():
        fetch(slot_ref, cur)

    @pl.when(i + 1 < pl.num_programs(0))
    def _():
        fetch(next_slot_ref, 1 - cur)

    for kk in range(TOP_K):
        pltpu.make_async_copy(y_ref.at[pl.ds(0, tt), :], buf_ref.at[cur, kk], sem.at[cur]).wait()
    route = route_ref[...]
    x = h_ref[...] + (buf_ref[cur, 0] * route[:, TOP_K:TOP_K + 1] + buf_ref[cur, 1] * route[:, TOP_K + 1:TOP_K + 2])
    y = x * lax.rsqrt(jnp.mean(x * x, axis=-1, keepdims=True) + NORM_EPS)
    o_ref[...] = y * g_ref[...]


def _combine_norm(h, y, slots, route, gain, *, n_rows, tt=256):
    d = h.shape[1]
    steps = n_rows // tt
    slots = slots.reshape(steps, 1, TOP_K * tt)
    return pl.pallas_call(
        functools.partial(_combine_kernel, tt=tt),
        grid=(steps,),
        in_specs=[pl.BlockSpec((1, 1, TOP_K * tt), lambda i: (i, 0, 0), memory_space=pltpu.SMEM),
                  pl.BlockSpec((1, 1, TOP_K * tt), lambda i: (jnp.minimum(i + 1, steps - 1), 0, 0),
                               memory_space=pltpu.SMEM),
                  pl.BlockSpec((tt, d), lambda i: (i, 0)),
                  pl.BlockSpec((tt, LANES), lambda i: (i, 0)),
                  pl.BlockSpec((1, d), lambda i: (0, 0)),
                  pl.BlockSpec(memory_space=pl.ANY)],
        out_specs=pl.BlockSpec((tt, d), lambda i: (i, 0)),
        out_shape=jax.ShapeDtypeStruct((n_rows, d), F32),
        scratch_shapes=[pltpu.VMEM((2, TOP_K, tt, d), F32), pltpu.SemaphoreType.DMA((2,))],
        compiler_params=_params("arbitrary"),
    )(slots, slots, h, route, gain.reshape(1, d), y)


def _even_layer(h, norm_mix, norm_ffn, w_in, w_out, pool_w, pool_scale, lb_logits, layer, hgrn_norm,
                w_gate, w_up, w_down, *, batch, n_real, seq):
    pool_width = pool_scale.shape[0]
    u = _rmsnorm(h, norm_mix)
    z = _matmul(u, w_in.astype(BF16), tn=1024, out_dtype=F32)
    y = _pool_mixer(z, pool_w.astype(BF16), pool_scale, out_width=w_out.shape[0], n_real=n_real, seq=seq)
    y = _hgrn_mixer(z, pool_width, lb_logits, layer, hgrn_norm, y, pool_width, batch=batch, n_real=n_real, seq=seq)
    h = _matmul_residual(y, w_out.astype(BF16), h, tn=1024)
    u = _rmsnorm(h, norm_ffn)
    mid = _matmul_swiglu(u, w_gate, w_up, tn=256, tm=3 * ROW_TILE)
    return _matmul_residual(mid, w_down.astype(BF16), h, tn=512)


def _odd_layer(h, norm_mix, norm_ffn, final_norm, w_in, w_out, sinks, router, w_gate, w_up, w_down,
               *, batch, n_real, seq):
    d = h.shape[1]
    hd = ATT_HEAD_DIM
    n_q = sinks.shape[0]
    nq = n_q * hd
    nkv = ATT_KV_HEADS * hd
    u = _rmsnorm(h, norm_mix)
    w = w_in.astype(BF16)
    wk = w[:, nq:nq + nkv].reshape(d, ATT_KV_HEADS, hd)
    wv = w[:, nq + nkv:].reshape(d, ATT_KV_HEADS, hd)
    w_perm = jnp.concatenate([w[:, :nq], jnp.concatenate([wk, wv], axis=2).reshape(d, 2 * nkv)], axis=1)
    z = _matmul(u, w_perm, tn=1024, out_dtype=BF16)
    slopes = 2.0 ** (-8.0 * jnp.arange(1, n_q + 1, dtype=F32) / n_q)
    att = _attention(z, slopes, sinks.astype(F32), batch=batch, n_real=n_real, seq=seq)
    h = _matmul_residual(att, w_out.astype(BF16), h, tn=1024, m_rows=n_real)
    return _moe_ffn_norm(h, norm_ffn, final_norm, router, w_gate, w_up, w_down, n_real=n_real)


def _moe_ffn_norm(h, norm_ffn, final_norm, router, w_gate, w_up, w_down, *, n_real):
    u, route, counts = _route(h, norm_ffn, router, n_rows=n_real)
    blk = ROW_TILE
    n_blocks = n_real * TOP_K // blk + N_EXPERTS
    e_idx = route[:, 0:TOP_K].astype(jnp.int32)
    rank = route[:, 2 * TOP_K:3 * TOP_K].astype(jnp.int32)
    cnt = counts[0, :N_EXPERTS].astype(jnp.int32)
    padded = (cnt + blk - 1) // blk * blk
    pad_end = jnp.cumsum(padded)
    pad_start = pad_end - padded
    slots = pad_start[e_idx] + rank
    n_used = (pad_end[-1] // blk).astype(jnp.int32).reshape(1)
    blocks = jnp.minimum(jnp.arange(n_blocks, dtype=jnp.int32), n_used[0] - 1)
    block_expert = jnp.minimum(jnp.searchsorted(pad_end, blocks * blk, side='right'),
                               N_EXPERTS - 1).astype(jnp.int32)
    xs = _dispatch(u, slots, n_blocks * blk)
    mid = _moe_up(xs, block_expert, n_used, w_gate, w_up, tn=512)
    y = _moe_down(mid, block_expert, n_used, w_down, tn=min(512, h.shape[1]))
    return _combine_norm(h, y, slots, route, final_norm, n_rows=n_real)


def kernel(x, meta_tokens, norm_mix, norm_ffn, final_norm, even_w_in, even_w_out, pool_w_group, pool_scale, hgrn_lb_logits, hgrn_norm, odd_w_in, odd_w_out, attn_sinks, ffn_w_gate, ffn_w_up, ffn_w_down, moe_router, moe_w_gate, moe_w_up, moe_w_down):
    batch, seq, d = x.shape
    n_real = batch * seq
    assert seq % ROW_TILE == 0 and (batch * META_BLOCK) % ROW_TILE == 0
    assert norm_mix.shape[0] == 2, "one even and one odd layer"
    meta_block = jnp.concatenate([jnp.zeros((META_BLOCK - N_META, d), F32), meta_tokens.astype(F32)], axis=0)
    h = (x.reshape(n_real, d), jnp.tile(meta_block, (batch, 1)))
    dims = dict(batch=batch, n_real=n_real, seq=seq)
    h = _even_layer(h, norm_mix[0], norm_ffn[0], even_w_in[0], even_w_out[0], pool_w_group[0],
                    pool_scale[0], hgrn_lb_logits, 0, hgrn_norm[0], ffn_w_gate[0], ffn_w_up[0],
                    ffn_w_down[0], **dims)
    out = _odd_layer(h, norm_mix[1], norm_ffn[1], final_norm, odd_w_in[0], odd_w_out[0], attn_sinks[0],
                     moe_router[0], moe_w_gate[0], moe_w_up[0], moe_w_down[0], **dims)
    return out.reshape(batch, seq, d)
```

```python
import functools
import math

import numpy as np
import jax
import jax.numpy as jnp
from jax import lax
from jax.experimental import pallas as pl
from jax.experimental.pallas import tpu as pltpu

F32 = jnp.float32
BF16 = jnp.bfloat16

N_META = 16
NORM_EPS = 1e-5
POOL_WINDOWS = (2, 4, 8, 16)
POOL_HALO = 16
HGRN_HEAD_DIM = 128
HGRN_CHUNK = 128
HGRN_HEADS_PER_STEP = 16
HGRN_TABLE_LEVELS = 3
ATT_HEAD_DIM = 64
ATT_KV_HEADS = 8
ATT_BLOCK = 128
ATT_KV_PER_STEP = 8
N_EXPERTS = 8
TOP_K = 2

META_BLOCK = 128
ROW_TILE = 512
LANES = 128
ROWS_PER_TRIP = 8
VMEM_LIMIT = 56 * 1024 * 1024


def _params(*sem):
    return pltpu.CompilerParams(dimension_semantics=sem, vmem_limit_bytes=VMEM_LIMIT)


def _silu(x):
    return x * jax.nn.sigmoid(x)


def _dot(a, b):
    return jnp.dot(a, b, preferred_element_type=F32)


def _dot_nt(a, b):
    return lax.dot_general(a, b, (((1,), (1,)), ((), ())), preferred_element_type=F32)


def _dot_tn(a, b):
    return lax.dot_general(a, b, (((0,), (0,)), ((), ())), preferred_element_type=F32)


def _row_source(h, tm, width, col):
    if not isinstance(h, tuple):
        return [h], [pl.BlockSpec((tm, width), lambda *ids: (ids[0], col(*ids)))], None
    head, tail = h
    assert tail.shape[0] == tm and head.shape[0] % tm == 0
    head_tiles = head.shape[0] // tm
    specs = [pl.BlockSpec((tm, width), lambda *ids: (jnp.minimum(ids[0], head_tiles - 1), col(*ids))),
             pl.BlockSpec((tm, width), lambda *ids: (0, col(*ids)))]
    return [head, tail], specs, head_tiles


def _read_rows(refs, head_tiles):
    if head_tiles is None:
        return refs[0][...]
    return jnp.where(pl.program_id(0) < head_tiles, refs[0][...], refs[1][...])


def _rmsnorm_kernel(*refs, head_tiles):
    *h_refs, g_ref, o_ref = refs
    x = _read_rows(h_refs, head_tiles)
    y = x * lax.rsqrt(jnp.mean(x * x, axis=-1, keepdims=True) + NORM_EPS)
    o_ref[...] = (y * g_ref[...]).astype(o_ref.dtype)


def _rmsnorm(h, gain, out_dtype=BF16, tm=ROW_TILE):
    d = gain.shape[0]
    arrays, specs, head_tiles = _row_source(h, tm, d, lambda *ids: 0)
    m = sum(a.shape[0] for a in arrays)
    return pl.pallas_call(
        functools.partial(_rmsnorm_kernel, head_tiles=head_tiles),
        grid=(m // tm,),
        in_specs=[*specs, pl.BlockSpec((1, d), lambda i: (0, 0))],
        out_specs=pl.BlockSpec((tm, d), lambda i: (i, 0)),
        out_shape=jax.ShapeDtypeStruct((m, d), out_dtype),
        compiler_params=_params("arbitrary"),
    )(*arrays, gain.reshape(1, d))


def _mm_kernel(a_ref, w_ref, o_ref):
    o_ref[...] = _dot(a_ref[...], w_ref[...]).astype(o_ref.dtype)


def _mm_res_kernel(a_ref, w_ref, *refs, head_tiles):
    *r_refs, o_ref = refs
    o_ref[...] = _read_rows(r_refs, head_tiles) + _dot(a_ref[...], w_ref[...])


def _mm_swiglu_kernel(a_ref, wg_ref, wu_ref, o_ref, wgb_ref, wub_ref):
    @pl.when(pl.program_id(1) == 0)
    def _():
        wgb_ref[...] = wg_ref[...].astype(BF16)
        wub_ref[...] = wu_ref[...].astype(BF16)

    a = a_ref[...]
    o_ref[...] = (_silu(_dot(a, wgb_ref[...])) * _dot(a, wub_ref[...])).astype(o_ref.dtype)


def _matmul(a, w, *, tn, out_dtype, tm=ROW_TILE):
    m, k = a.shape
    n = w.shape[1]
    return pl.pallas_call(
        _mm_kernel,
        grid=(n // tn, m // tm),
        in_specs=[pl.BlockSpec((tm, k), lambda j, i: (i, 0)), pl.BlockSpec((k, tn), lambda j, i: (0, j))],
        out_specs=pl.BlockSpec((tm, tn), lambda j, i: (i, j)),
        out_shape=jax.ShapeDtypeStruct((m, n), out_dtype),
        compiler_params=_params("arbitrary", "arbitrary"),
    )(a, w)


def _matmul_residual(a, w, res, *, tn, tm=ROW_TILE, m_rows=None):
    m = a.shape[0] if m_rows is None else m_rows
    k = a.shape[1]
    n = w.shape[1]
    res_arrays, res_specs, head_tiles = _row_source(res, tm, tn, lambda i, j: j)
    return pl.pallas_call(
        functools.partial(_mm_res_kernel, head_tiles=head_tiles),
        grid=(m // tm, n // tn),
        in_specs=[pl.BlockSpec((tm, k), lambda i, j: (i, 0)),
                  pl.BlockSpec((k, tn), lambda i, j: (0, j)),
                  *res_specs],
        out_specs=pl.BlockSpec((tm, tn), lambda i, j: (i, j)),
        out_shape=jax.ShapeDtypeStruct((m, n), F32),
        compiler_params=_params("arbitrary", "arbitrary"),
    )(a, w, *res_arrays)


def _matmul_swiglu(a, wg, wu, *, tn, tm=ROW_TILE):
    m, k = a.shape
    n = wg.shape[1]
    return pl.pallas_call(
        _mm_swiglu_kernel,
        grid=(n // tn, m // tm),
        in_specs=[pl.BlockSpec((tm, k), lambda j, i: (i, 0)),
                  pl.BlockSpec((k, tn), lambda j, i: (0, j)),
                  pl.BlockSpec((k, tn), lambda j, i: (0, j))],
        out_specs=pl.BlockSpec((tm, tn), lambda j, i: (i, j)),
        out_shape=jax.ShapeDtypeStruct((m, n), BF16),
        scratch_shapes=[pltpu.VMEM((k, tn), BF16), pltpu.VMEM((k, tn), BF16)],
        compiler_params=_params("arbitrary", "arbitrary"),
    )(a, wg, wu)


def _pool_kernel(z_ref, halo_ref, wg_ref, sc_ref, o_ref, *, tm, n_real, group):
    i = pl.program_id(0)
    rows = i * tm + lax.broadcasted_iota(jnp.int32, (tm, 1), 0)
    in_meta = rows >= n_real
    pos = (rows % META_BLOCK) - (META_BLOCK - N_META)
    is_pad = in_meta & (pos < 0)
    for gi, w in enumerate(POOL_WINDOWS):
        cols = slice(gi * group, (gi + 1) * group)
        x = z_ref[:, cols]
        s = jnp.concatenate([halo_ref[:, cols], x], axis=0)
        step = 1
        while step < w:
            s = s + pltpu.roll(s, step, axis=0)
            step *= 2
        count = jnp.where(in_meta, jnp.clip(pos + 1, 1, w), w).astype(F32)
        d = s[POOL_HALO:] / count - x
        y = _dot(d.astype(BF16), wg_ref[gi]) * sc_ref[:, cols]
        o_ref[:, cols] = jnp.where(is_pad, 0.0, y).astype(o_ref.dtype)
    rest = o_ref.shape[1] - len(POOL_WINDOWS) * group
    o_ref[:, len(POOL_WINDOWS) * group:] = jnp.zeros((tm, rest), o_ref.dtype)


def _pool_mixer(z, w_group, scale, *, out_width, n_real, seq, tm=ROW_TILE):
    m = z.shape[0]
    n_g, group, _ = w_group.shape
    width = n_g * group
    tiles_per_seq = seq // tm
    hb = tm // POOL_HALO

    def halo_index(i):
        b = i // tiles_per_seq
        meta_tail = (n_real + b * META_BLOCK + META_BLOCK - POOL_HALO) // POOL_HALO
        real = jnp.where(i % tiles_per_seq == 0, meta_tail, i * hb - 1)
        return (jnp.where(i * tm >= n_real, n_real // POOL_HALO, real), 0)

    return pl.pallas_call(
        functools.partial(_pool_kernel, tm=tm, n_real=n_real, group=group),
        grid=(m // tm,),
        in_specs=[pl.BlockSpec((tm, width), lambda i: (i, 0)),
                  pl.BlockSpec((POOL_HALO, width), halo_index),
                  pl.BlockSpec((n_g, group, group), lambda i: (0, 0, 0)),
                  pl.BlockSpec((1, width), lambda i: (0, 0))],
        out_specs=pl.BlockSpec((tm, out_width), lambda i: (i, 0)),
        out_shape=jax.ShapeDtypeStruct((m, out_width), BF16),
        compiler_params=_params("arbitrary"),
    )(z, z, w_group, scale.reshape(1, width))


def _hgrn_tables(c):
    levels = int(math.log2(c))
    t = np.arange(c)
    blocks = [(t[None, :] <= t[:, None])]
    for l in range(1, HGRN_TABLE_LEVELS + 1):
        bs, hs = 1 << l, 1 << (l - 1)
        mid = (t // bs) * bs + hs - 1
        upper = (t % bs) >= hs
        eq = upper[:, None] & (t[None, :] > mid[:, None]) & (t[None, :] <= t[:, None])
        ek = (~upper)[:, None] & (t[None, :] > t[:, None]) & (t[None, :] <= mid[:, None])
        blocks.append(eq | ek)
    sums = np.concatenate(blocks, axis=0).astype(np.float32)
    sums = np.concatenate([sums, sums], axis=1)
    x = t[:, None] ^ t[None, :]
    lev = np.where(t[None, :] > t[:, None], -1,
                   np.floor(np.log2(np.maximum(x, 1))).astype(np.int32) + (x > 0))
    return sums, lev.astype(np.int32), levels


def _hgrn_kernel(q_ref, f_ref, i_ref, g_ref, lbl_ref, gain_ref, sums_ref, lev_ref, y_in_ref, o_ref, state_ref,
                 *, c, levels, heads, layer):
    del y_in_ref
    @pl.when(pl.program_id(2) == 0)
    def _():
        state_ref[...] = jnp.zeros_like(state_ref)

    lev = lev_ref[...]
    hd = HGRN_HEAD_DIM
    head_cols = [slice(h * hd, (h + 1) * hd) for h in range(heads)]
    lbl = lbl_ref[...]
    ex = jnp.exp(lbl - lbl.max(0, keepdims=True))
    lb = ex[0:layer + 1].sum(0, keepdims=True) / ex.sum(0, keepdims=True)
    fg = lb + (1.0 - lb) * jax.nn.sigmoid(f_ref[...])
    log2_f = jnp.log2(fg)
    k = 1.0 - fg
    q = _silu(q_ref[...])
    v = i_ref[...].astype(BF16)
    hi = log2_f.astype(BF16)
    r1 = log2_f - hi.astype(F32)
    mid = r1.astype(BF16)
    lo = (r1 - mid.astype(F32)).astype(BF16)
    e = (_dot(sums_ref[...], jnp.concatenate([hi, mid], axis=0))
         + _dot(sums_ref[:, 0:c], lo))
    cum = e[0:c]
    total = cum[c - 1:c, :]
    q_dec = (q * jnp.exp2(cum)).astype(BF16)
    k_dec = (k * jnp.exp2(total - cum)).astype(BF16)
    carry = jnp.exp2(total)
    states = [state_ref[h] for h in range(heads)]
    o = [_dot_nt(q_dec[:, cs], st.astype(BF16)) for cs, st in zip(head_cols, states)]
    for h, (cs, st) in enumerate(zip(head_cols, states)):
        state_ref[h] = st * carry[:, cs] + _dot_tn(v[:, cs], k_dec[:, cs])
    q_l, k_l = q.astype(BF16), k.astype(BF16)
    scores = [jnp.where(lev == 0, _dot_nt(q_l[:, cs], k_l[:, cs]), 0.0) for cs in head_cols]
    def level_sums(l):
        if l <= HGRN_TABLE_LEVELS:
            return e[l * c:(l + 1) * c]
        bs, hs = 1 << l, 1 << (l - 1)
        parts = []
        for start in range(0, c, bs):
            middle = cum[start + hs - 1:start + hs, :]
            parts += [middle - cum[start:start + hs], cum[start + hs:start + bs] - middle]
        return jnp.concatenate(parts, axis=0)

    for l in range(1, levels + 1):
        dec = jnp.exp2(level_sums(l))
        q_l, k_l = (q * dec).astype(BF16), (k * dec).astype(BF16)
        scores = [jnp.where(lev == l, _dot_nt(q_l[:, cs], k_l[:, cs]), s) for s, cs in zip(scores, head_cols)]
    o = [oh + _dot(s.astype(BF16), v[:, cs]) for oh, s, cs in zip(o, scores, head_cols)]
    o = [oh * lax.rsqrt(jnp.mean(oh * oh, axis=-1, keepdims=True) + NORM_EPS) for oh in o]
    o_ref[...] = (jnp.concatenate(o, axis=1) * gain_ref[...] * _silu(g_ref[...])).astype(o_ref.dtype)


def _hgrn_mixer(z, col0, lb_logits, layer, norm_gain, y, y_col0, *, batch, n_real, seq):
    m = z.shape[0]
    n_lb, width = lb_logits.shape
    c = HGRN_CHUNK
    assert META_BLOCK == c
    hps = HGRN_HEADS_PER_STEP
    wb = hps * HGRN_HEAD_DIM
    n_hg = width // wb
    chunks = seq // c + 1
    sums, lev, levels = _hgrn_tables(c)

    def row_block(b, ci):
        return jnp.where(ci == 0, n_real // c + b, b * (seq // c) + ci - 1)

    def zspec(r):
        cb = (col0 + r * width) // wb
        return pl.BlockSpec((c, wb), lambda b, hg, ci: (row_block(b, ci), cb + hg))

    vec = pl.BlockSpec((1, wb), lambda b, hg, ci: (0, hg))
    return pl.pallas_call(
        functools.partial(_hgrn_kernel, c=c, levels=levels, heads=hps, layer=layer),
        grid=(batch, n_hg, chunks),
        in_specs=[zspec(0), zspec(1), zspec(2), zspec(3),
                  pl.BlockSpec((n_lb, wb), lambda b, hg, ci: (0, hg)), vec,
                  pl.BlockSpec(sums.shape, lambda b, hg, ci: (0, 0)),
                  pl.BlockSpec(lev.shape, lambda b, hg, ci: (0, 0)),
                  pl.BlockSpec(memory_space=pl.ANY)],
        out_specs=pl.BlockSpec((c, wb), lambda b, hg, ci: (row_block(b, ci), y_col0 // wb + hg)),
        out_shape=jax.ShapeDtypeStruct(y.shape, y.dtype),
        scratch_shapes=[pltpu.VMEM((hps, HGRN_HEAD_DIM, HGRN_HEAD_DIM), F32)],
        input_output_aliases={8: 0},
        compiler_params=_params("arbitrary", "arbitrary", "arbitrary"),
    )(z, z, z, z, lb_logits.astype(F32), norm_gain.reshape(1, width),
      jnp.asarray(sums, BF16), jnp.asarray(lev), y)


def _attn_kernel(slope_ref, sink_ref, q_ref, kvc_ref, kvp_ref, kvm_ref, o_ref, *, group, kvs):
    i = pl.program_id(1)
    j = pl.program_id(2)
    hd = ATT_HEAD_DIM
    blk = ATT_BLOCK
    pairs = group // 2
    blocks = kvs * pairs
    low = lax.broadcasted_iota(jnp.int32, (1, 2 * hd), 1) < hd

    def block_diag(kv):
        vk = jnp.concatenate([kv[:, hd:], kv[:, :hd]], axis=1)
        zero = jnp.zeros_like(kv)
        keys = jnp.concatenate([jnp.where(low, kv, zero), jnp.where(low, zero, vk)], axis=0)
        vals = jnp.concatenate([jnp.where(low, vk, zero), jnp.where(low, zero, kv)], axis=0)
        return keys, vals

    def with_ones(vals, n_first):
        shape = (vals.shape[0], 2 * hd)
        first = lax.broadcasted_iota(jnp.int32, shape, 0) < n_first
        ones = jnp.where(first == (lax.broadcasted_iota(jnp.int32, shape, 1) < hd), 1.0, 0.0)
        return jnp.concatenate([vals, ones.astype(BF16)], axis=1)

    pad = jnp.zeros((2 * hd - 2 * N_META, 2 * hd), BF16)
    s, sm, v_band, v_meta = [], [], [], []
    for a in range(kvs):
        cols = slice(a * 2 * hd, (a + 1) * 2 * hd)
        kb, vb = block_diag(jnp.concatenate([kvp_ref[:, cols], kvc_ref[:, cols]], axis=0))
        km, vm = block_diag(kvm_ref[META_BLOCK - N_META:, cols])
        q2 = jnp.concatenate([q_ref[:, (a * pairs + p) * 2 * hd:(a * pairs + p + 1) * 2 * hd]
                              for p in range(pairs)], axis=0)
        q2 = (q2.astype(F32) * hd ** -0.5).astype(BF16)
        s.append(_dot_nt(q2, kb))
        sm.append(_dot_nt(q2, jnp.concatenate([km, pad], axis=0)))
        v_band.append(with_ones(vb, 2 * blk))
        v_meta.append(with_ones(jnp.concatenate([vm, pad], axis=0), N_META))
    s = jnp.concatenate(s, axis=0)
    sm = jnp.concatenate(sm, axis=0)

    r = lax.broadcasted_iota(jnp.int32, (blk, 2 * blk), 0)
    cidx = lax.broadcasted_iota(jnp.int32, (blk, 2 * blk), 1)
    dist = r - cidx + blk
    band_ok = (dist >= 0) & (dist < blk) & ((cidx >= blk) | (i > 0))
    neg_dist = jnp.where(band_ok, -dist.astype(F32), -jnp.inf)
    neg_dist = jnp.concatenate([neg_dist] * blocks, axis=0)

    def per_row(ref, half):
        return jnp.concatenate([jnp.full((blk, 1), ref[j * kvs * group + 2 * p + half], F32)
                                for p in range(blocks)], axis=0)

    mlane = lax.broadcasted_iota(jnp.int32, (1, 2 * hd), 1)
    halves = []
    for half in range(2):
        sink = per_row(sink_ref, half)
        lb = s[:, half * 2 * blk:(half + 1) * 2 * blk] + per_row(slope_ref, half) * neg_dist
        lm = jnp.where((mlane >= half * N_META) & (mlane < (half + 1) * N_META), sm, -jnp.inf)
        mx = jnp.maximum(jnp.maximum(jnp.maximum(lb[:, :blk], lb[:, blk:]), lm).max(-1, keepdims=True), sink)
        halves.append((jnp.exp(lb - mx), jnp.exp(lm - mx), jnp.exp(sink - mx)))
    pb = jnp.concatenate([halves[0][0], halves[1][0]], axis=1).astype(BF16)
    pm = (halves[0][1] + halves[1][1]).astype(BF16)
    rows = [slice(a * pairs * blk, (a + 1) * pairs * blk) for a in range(kvs)]
    acc = jnp.concatenate([_dot(pb[rs], vb) + _dot(pm[rs], vm) for rs, vb, vm in zip(rows, v_band, v_meta)],
                          axis=0)
    o = acc[:, :2 * hd] / (acc[:, 2 * hd:] + jnp.where(low, halves[0][2], halves[1][2]))
    for p in range(blocks):
        o_ref[:, p * 2 * hd:(p + 1) * 2 * hd] = o[p * blk:(p + 1) * blk].astype(o_ref.dtype)


def _attention(z, slopes, sinks, *, batch, n_real, seq):
    hd = ATT_HEAD_DIM
    n_q = slopes.shape[0]
    group = n_q // ATT_KV_HEADS
    kvs = ATT_KV_PER_STEP
    qw = kvs * group * hd
    kvw = kvs * 2 * hd
    kv0 = n_q * hd // kvw
    nblk = seq // ATT_BLOCK
    smem = pl.BlockSpec(memory_space=pltpu.SMEM)
    kv_spec = lambda rows: pl.BlockSpec((ATT_BLOCK, kvw), lambda b, i, j: (rows(b, i), kv0 + j))
    return pl.pallas_call(
        functools.partial(_attn_kernel, group=group, kvs=kvs),
        grid=(batch, nblk, ATT_KV_HEADS // kvs),
        in_specs=[smem, smem,
                  pl.BlockSpec((ATT_BLOCK, qw), lambda b, i, j: (b * nblk + i, j)),
                  kv_spec(lambda b, i: b * nblk + i),
                  kv_spec(lambda b, i: b * nblk + jnp.maximum(i - 1, 0)),
                  kv_spec(lambda b, i: n_real // ATT_BLOCK + b)],
        out_specs=pl.BlockSpec((ATT_BLOCK, qw), lambda b, i, j: (b * nblk + i, j)),
        out_shape=jax.ShapeDtypeStruct((n_real, n_q * hd), BF16),
        compiler_params=_params("arbitrary", "arbitrary", "arbitrary"),
    )(slopes, sinks, z, z, z, z)


def _route_kernel(h_ref, g_ref, r_ref, u_ref, route_ref, count_ref, carry_ref, *, tm):
    @pl.when(pl.program_id(0) == 0)
    def _():
        carry_ref[...] = jnp.zeros_like(carry_ref)

    x = h_ref[...]
    u = x * lax.rsqrt(jnp.mean(x * x, axis=-1, keepdims=True) + NORM_EPS) * g_ref[...]
    bits = lax.bitcast_convert_type(u.astype(BF16).astype(F32), jnp.uint32)
    half = bits.shape[1] // 2
    u_ref[...] = (bits[:, :half] >> 16) | (bits[:, half:] & jnp.uint32(0xFFFF0000))
    u_hi = u.astype(BF16)
    rem = u - u_hi.astype(F32)
    u_mid = rem.astype(BF16)
    u_lo = (rem - u_mid.astype(F32)).astype(BF16)
    router = r_ref[...]
    parts = _dot(u_hi, router) + _dot(u_mid, router) + _dot(u_lo, router)
    logits = (parts + pltpu.roll(parts, LANES - N_EXPERTS, axis=1)
              + pltpu.roll(parts, LANES - 2 * N_EXPERTS, axis=1))
    lane = lax.broadcasted_iota(jnp.int32, (tm, LANES), 1)
    lg = jnp.where(lane < N_EXPERTS, logits, -jnp.inf)
    m1 = lg.max(-1, keepdims=True)
    i1 = jnp.where(lg == m1, lane, LANES).min(-1, keepdims=True)
    lg2 = jnp.where(lane == i1, -jnp.inf, lg)
    m2 = lg2.max(-1, keepdims=True)
    i2 = jnp.where(lg2 == m2, lane, LANES).min(-1, keepdims=True)
    e2 = jnp.exp(m2 - m1)
    g1 = 1.0 / (1.0 + e2)
    g2 = e2 / (1.0 + e2)
    chosen = (lane == i1) | (lane == i2)
    onehot = jnp.where(chosen, 1.0, 0.0)
    rr = lax.broadcasted_iota(jnp.int32, (tm, tm), 0)
    cc = lax.broadcasted_iota(jnp.int32, (tm, tm), 1)
    before = jnp.where(cc < rr, 1.0, 0.0).astype(BF16)
    rank = _dot(before, onehot.astype(BF16)) + carry_ref[...]
    r1 = jnp.where(lane == i1, rank, 0.0).sum(-1, keepdims=True)
    r2 = jnp.where(lane == i2, rank, 0.0).sum(-1, keepdims=True)
    carry_ref[...] = carry_ref[...] + onehot.sum(0, keepdims=True)
    count_ref[...] = carry_ref[...]
    packed = jnp.where(lane == 0, i1.astype(F32), 0.0)
    for idx, val in ((1, i2.astype(F32)), (2, g1), (3, g2), (4, r1), (5, r2)):
        packed = jnp.where(lane == idx, val, packed)
    route_ref[...] = packed


def _route(h, gain, router, *, n_rows, tm=ROW_TILE):
    d = h.shape[1]
    r_hi = router.astype(BF16)
    rem = router.astype(F32) - r_hi.astype(F32)
    r_mid = rem.astype(BF16)
    r_lo = (rem - r_mid.astype(F32)).astype(BF16)
    router_pad = jnp.pad(jnp.concatenate([r_hi, r_mid, r_lo], axis=1), ((0, 0), (0, LANES - 3 * N_EXPERTS)))
    return pl.pallas_call(
        functools.partial(_route_kernel, tm=tm),
        grid=(n_rows // tm,),
        in_specs=[pl.BlockSpec((tm, d), lambda i: (i, 0)),
                  pl.BlockSpec((1, d), lambda i: (0, 0)),
                  pl.BlockSpec((d, LANES), lambda i: (0, 0))],
        out_specs=[pl.BlockSpec((tm, d // 2), lambda i: (i, 0)),
                   pl.BlockSpec((tm, LANES), lambda i: (i, 0)),
                   pl.BlockSpec((1, LANES), lambda i: (0, 0))],
        out_shape=[jax.ShapeDtypeStruct((n_rows, d // 2), jnp.uint32),
                   jax.ShapeDtypeStruct((n_rows, LANES), F32),
                   jax.ShapeDtypeStruct((1, LANES), F32)],
        scratch_shapes=[pltpu.VMEM((1, LANES), F32)],
        compiler_params=_params("arbitrary"),
    )(h, gain.reshape(1, d), router_pad)


def _dispatch_kernel(slot_ref, u_ref, xs_in_ref, xs_ref, sem, *, tt):
    del xs_in_ref

    def copy(t, kk):
        return pltpu.make_async_copy(u_ref.at[pl.ds(t, 1), :],
                                     xs_ref.at[pl.ds(slot_ref[0, 0, TOP_K * t + kk], 1), :], sem)

    def issue(g, carry):
        base = pl.multiple_of(g * ROWS_PER_TRIP, ROWS_PER_TRIP)
        for r in range(ROWS_PER_TRIP):
            for kk in range(TOP_K):
                copy(base + r, kk).start()
        return carry

    lax.fori_loop(0, tt // ROWS_PER_TRIP, issue, 0)
    for _ in range(TOP_K):
        pltpu.make_async_copy(u_ref, xs_ref.at[pl.ds(0, tt), :], sem).wait()


def _dispatch(u, slots, n_slots, *, tt=ROW_TILE):
    n, d = u.shape
    return pl.pallas_call(
        functools.partial(_dispatch_kernel, tt=tt),
        grid=(n // tt,),
        in_specs=[pl.BlockSpec((1, 1, TOP_K * tt), lambda i: (i, 0, 0), memory_space=pltpu.SMEM),
                  pl.BlockSpec((tt, d), lambda i: (i, 0)),
                  pl.BlockSpec(memory_space=pl.ANY)],
        out_specs=pl.BlockSpec(memory_space=pl.ANY),
        out_shape=jax.ShapeDtypeStruct((n_slots, d), u.dtype),
        scratch_shapes=[pltpu.SemaphoreType.DMA(())],
        input_output_aliases={2: 0},
        compiler_params=_params("arbitrary"),
    )(slots.reshape(n // tt, 1, TOP_K * tt), u, jnp.zeros((n_slots, d), u.dtype))


def _new_expert(be_ref, i):
    return (i == 0) | (be_ref[i] != be_ref[jnp.maximum(i - 1, 0)])


def _moe_up_kernel(be_ref, nb_ref, a_ref, wg_ref, wu_ref, o_ref, wgb_ref, wub_ref):
    i = pl.program_id(1)

    @pl.when(_new_expert(be_ref, i))
    def _():
        wgb_ref[...] = wg_ref[0].astype(BF16)
        wub_ref[...] = wu_ref[0].astype(BF16)

    @pl.when(i < nb_ref[0])
    def _():
        packed = a_ref[...]
        half = packed.shape[1]
        a_lo = lax.bitcast_convert_type(packed << 16, F32).astype(BF16)
        a_hi = lax.bitcast_convert_type(packed & jnp.uint32(0xFFFF0000), F32).astype(BF16)
        gate = _dot(a_lo, wgb_ref[:half, :]) + _dot(a_hi, wgb_ref[half:, :])
        up = _dot(a_lo, wub_ref[:half, :]) + _dot(a_hi, wub_ref[half:, :])
        o_ref[...] = (_silu(gate) * up).astype(o_ref.dtype)

    @pl.when(i >= nb_ref[0])
    def _():
        o_ref[...] = jnp.zeros_like(o_ref)


def _moe_up(xs, block_expert, n_used, wg, wu, *, tn, tm=ROW_TILE):
    ns, kp = xs.shape
    k, n = wg.shape[1:]
    assert k == 2 * kp
    rows = lambda j, i, be, nb: (jnp.minimum(i, nb[0] - 1), 0)
    wspec = pl.BlockSpec((1, k, tn), lambda j, i, be, nb: (be[i], 0, j))
    return pl.pallas_call(
        _moe_up_kernel,
        grid_spec=pltpu.PrefetchScalarGridSpec(
            num_scalar_prefetch=2,
            grid=(n // tn, ns // tm),
            in_specs=[pl.BlockSpec((tm, kp), rows), wspec, wspec],
            out_specs=pl.BlockSpec((tm, tn), lambda j, i, be, nb: (i, j)),
            scratch_shapes=[pltpu.VMEM((k, tn), BF16), pltpu.VMEM((k, tn), BF16)]),
        out_shape=jax.ShapeDtypeStruct((ns, n), BF16),
        compiler_params=_params("arbitrary", "arbitrary"),
    )(block_expert, n_used, xs, wg, wu)


def _moe_down_kernel(be_ref, nb_ref, a_ref, w_ref, o_ref, wb_ref):
    i = pl.program_id(1)

    @pl.when(_new_expert(be_ref, i))
    def _():
        wb_ref[...] = w_ref[0].astype(BF16)

    @pl.when(i < nb_ref[0])
    def _():
        o_ref[...] = _dot(a_ref[...], wb_ref[...])

    @pl.when(i >= nb_ref[0])
    def _():
        o_ref[...] = jnp.zeros_like(o_ref)


def _moe_down(hmid, block_expert, n_used, w, *, tn, tm=ROW_TILE):
    ns, k = hmid.shape
    n = w.shape[2]
    return pl.pallas_call(
        _moe_down_kernel,
        grid_spec=pltpu.PrefetchScalarGridSpec(
            num_scalar_prefetch=2,
            grid=(n // tn, ns // tm),
            in_specs=[pl.BlockSpec((tm, k), lambda j, i, be, nb: (jnp.minimum(i, nb[0] - 1), 0)),
                      pl.BlockSpec((1, k, tn), lambda j, i, be, nb: (be[i], 0, j))],
            out_specs=pl.BlockSpec((tm, tn), lambda j, i, be, nb: (i, j)),
            scratch_shapes=[pltpu.VMEM((k, tn), BF16)]),
        out_shape=jax.ShapeDtypeStruct((ns, n), F32),
        compiler_params=_params("arbitrary", "arbitrary"),
    )(block_expert, n_used, hmid, w)


def _combine_kernel(slot_ref, next_slot_ref, h_ref, route_ref, g_ref, y_ref, o_ref, buf_ref, sem, *, tt):
    i = pl.program_id(0)
    cur = i % 2

    def copy(slots, buf, t, kk):
        return pltpu.make_async_copy(y_ref.at[pl.ds(slots[0, 0, TOP_K * t + kk], 1), :],
                                     buf_ref.at[buf, kk, pl.ds(t, 1), :], sem.at[buf])

    def fetch(slots, buf):
        def body(g, carry):
            base = pl.multiple_of(g * ROWS_PER_TRIP, ROWS_PER_TRIP)
            for r in range(ROWS_PER_TRIP):
                for kk in range(TOP_K):
                    copy(slots, buf, base + r, kk).start()
            return carry
        lax.fori_loop(0, tt // ROWS_PER_TRIP, body, 0)

    @pl.when(i == 0)
    def _():
        fetch(slot_ref, cur)

    @pl.when(i + 1 < pl.num_programs(0))
    def _():
        fetch(next_slot_ref, 1 - cur)

    for kk in range(TOP_K):
        pltpu.make_async_copy(y_ref.at[pl.ds(0, tt), :], buf_ref.at[cur, kk], sem.at[cur]).wait()
    route = route_ref[...]
    x = h_ref[...] + (buf_ref[cur, 0] * route[:, TOP_K:TOP_K + 1] + buf_ref[cur, 1] * route[:, TOP_K + 1:TOP_K + 2])
    y = x * lax.rsqrt(jnp.mean(x * x, axis=-1, keepdims=True) + NORM_EPS)
    o_ref[...] = y * g_ref[...]


def _combine_norm(h, y, slots, route, gain, *, n_rows, tt=256):
    d = h.shape[1]
    steps = n_rows // tt
    slots = slots.reshape(steps, 1, TOP_K * tt)
    return pl.pallas_call(
        functools.partial(_combine_kernel, tt=tt),
        grid=(steps,),
        in_specs=[pl.BlockSpec((1, 1, TOP_K * tt), lambda i: (i, 0, 0), memory_space=pltpu.SMEM),
                  pl.BlockSpec((1, 1, TOP_K * tt), lambda i: (jnp.minimum(i + 1, steps - 1), 0, 0),
                               memory_space=pltpu.SMEM),
                  pl.BlockSpec((tt, d), lambda i: (i, 0)),
                  pl.BlockSpec((tt, LANES), lambda i: (i, 0)),
                  pl.BlockSpec((1, d), lambda i: (0, 0)),
                  pl.BlockSpec(memory_space=pl.ANY)],
        out_specs=pl.BlockSpec((tt, d), lambda i: (i, 0)),
        out_shape=jax.ShapeDtypeStruct((n_rows, d), F32),
        scratch_shapes=[pltpu.VMEM((2, TOP_K, tt, d), F32), pltpu.SemaphoreType.DMA((2,))],
        compiler_params=_params("arbitrary"),
    )(slots, slots, h, route, gain.reshape(1, d), y)


def _even_layer(h, norm_mix, norm_ffn, w_in, w_out, pool_w, pool_scale, lb_logits, layer, hgrn_norm,
                w_gate, w_up, w_down, *, batch, n_real, seq):
    pool_width = pool_scale.shape[0]
    u = _rmsnorm(h, norm_mix)
    z = _matmul(u, w_in.astype(BF16), tn=1024, out_dtype=F32)
    y = _pool_mixer(z, pool_w.astype(BF16), pool_scale, out_width=w_out.shape[0], n_real=n_real, seq=seq)
    y = _hgrn_mixer(z, pool_width, lb_logits, layer, hgrn_norm, y, pool_width, batch=batch, n_real=n_real, seq=seq)
    h = _matmul_residual(y, w_out.astype(BF16), h, tn=1024)
    u = _rmsnorm(h, norm_ffn)
    mid = _matmul_swiglu(u, w_gate, w_up, tn=256, tm=3 * ROW_TILE)
    return _matmul_residual(mid, w_down.astype(BF16), h, tn=512)


def _odd_layer(h, norm_mix, norm_ffn, final_norm, w_in, w_out, sinks, router, w_gate, w_up, w_down,
               *, batch, n_real, seq):
    d = h.shape[1]
    hd = ATT_HEAD_DIM
    n_q = sinks.shape[0]
    nq = n_q * hd
    nkv = ATT_KV_HEADS * hd
    u = _rmsnorm(h, norm_mix)
    w = w_in.astype(BF16)
    wk = w[:, nq:nq + nkv].reshape(d, ATT_KV_HEADS, hd)
    wv = w[:, nq + nkv:].reshape(d, ATT_KV_HEADS, hd)
    w_perm = jnp.concatenate([w[:, :nq], jnp.concatenate([wk, wv], axis=2).reshape(d, 2 * nkv)], axis=1)
    z = _matmul(u, w_perm, tn=1024, out_dtype=BF16)
    slopes = 2.0 ** (-8.0 * jnp.arange(1, n_q + 1, dtype=F32) / n_q)
    att = _attention(z, slopes, sinks.astype(F32), batch=batch, n_real=n_real, seq=seq)
    h = _matmul_residual(att, w_out.astype(BF16), h, tn=1024, m_rows=n_real)
    return _moe_ffn_norm(h, norm_ffn, final_norm, router, w_gate, w_up, w_down, n_real=n_real)


def _moe_ffn_norm(h, norm_ffn, final_norm, router, w_gate, w_up, w_down, *, n_real):
    u, route, counts = _route(h, norm_ffn, router, n_rows=n_real)
    blk = ROW_TILE
    n_blocks = n_real * TOP_K // blk + N_EXPERTS
    e_idx = route[:, 0:TOP_K].astype(jnp.int32)
    rank = route[:, 2 * TOP_K:3 * TOP_K].astype(jnp.int32)
    cnt = counts[0, :N_EXPERTS].astype(jnp.int32)
    padded = (cnt + blk - 1) // blk * blk
    pad_end = jnp.cumsum(padded)
    pad_start = pad_end - padded
    slots = pad_start[e_idx] + rank
    n_used = (pad_end[-1] // blk).astype(jnp.int32).reshape(1)
    blocks = jnp.minimum(jnp.arange(n_blocks, dtype=jnp.int32), n_used[0] - 1)
    block_expert = jnp.minimum(jnp.searchsorted(pad_end, blocks * blk, side='right'),
                               N_EXPERTS - 1).astype(jnp.int32)
    xs = _dispatch(u, slots, n_blocks * blk)
    mid = _moe_up(xs, block_expert, n_used, w_gate, w_up, tn=512)
    y = _moe_down(mid, block_expert, n_used, w_down, tn=min(512, h.shape[1]))
    return _combine_norm(h, y, slots, route, final_norm, n_rows=n_real)


def kernel(x, meta_tokens, norm_mix, norm_ffn, final_norm, even_w_in, even_w_out, pool_w_group, pool_scale, hgrn_lb_logits, hgrn_norm, odd_w_in, odd_w_out, attn_sinks, ffn_w_gate, ffn_w_up, ffn_w_down, moe_router, moe_w_gate, moe_w_up, moe_w_down):
    batch, seq, d = x.shape
    n_real = batch * seq
    assert seq % ROW_TILE == 0 and (batch * META_BLOCK) % ROW_TILE == 0
    assert norm_mix.shape[0] == 2, "one even and one odd layer"
    meta_block = jnp.concatenate([jnp.zeros((META_BLOCK - N_META, d), F32), meta_tokens.astype(F32)], axis=0)
    h = (x.reshape(n_real, d), jnp.tile(meta_block, (batch, 1)))
    dims = dict(batch=batch, n_real=n_real, seq=seq)
    h = _even_layer(h, norm_mix[0], norm_ffn[0], even_w_in[0], even_w_out[0], pool_w_group[0],
                    pool_scale[0], hgrn_lb_logits, 0, hgrn_norm[0], ffn_w_gate[0], ffn_w_up[0],
                    ffn_w_down[0], **dims)
    out = _odd_layer(h, norm_mix[1], norm_ffn[1], final_norm, odd_w_in[0], odd_w_out[0], attn_sinks[0],
                     moe_router[0], moe_w_gate[0], moe_w_up[0], moe_w_down[0], **dims)
    return out.reshape(batch, seq, d)
```

```python
import functools
import math

import numpy as np
import jax
import jax.numpy as jnp
from jax import lax
from jax.experimental import pallas as pl
from jax.experimental.pallas import tpu as pltpu

F32 = jnp.float32
BF16 = jnp.bfloat16

N_META = 16
NORM_EPS = 1e-5
POOL_WINDOWS = (2, 4, 8, 16)
POOL_HALO = 16
HGRN_HEAD_DIM = 128
HGRN_CHUNK = 128
HGRN_HEADS_PER_STEP = 16
HGRN_TABLE_LEVELS = 3
ATT_HEAD_DIM = 64
ATT_KV_HEADS = 8
ATT_BLOCK = 128
ATT_KV_PER_STEP = 8
N_EXPERTS = 8
TOP_K = 2

META_BLOCK = 128
ROW_TILE = 512
LANES = 128
ROWS_PER_TRIP = 8
VMEM_LIMIT = 56 * 1024 * 1024


def _params(*sem):
    return pltpu.CompilerParams(dimension_semantics=sem, vmem_limit_bytes=VMEM_LIMIT)


def _silu(x):
    return x * jax.nn.sigmoid(x)


def _dot(a, b):
    return jnp.dot(a, b, preferred_element_type=F32)


def _dot_nt(a, b):
    return lax.dot_general(a, b, (((1,), (1,)), ((), ())), preferred_element_type=F32)


def _dot_tn(a, b):
    return lax.dot_general(a, b, (((0,), (0,)), ((), ())), preferred_element_type=F32)


def _row_source(h, tm, width, col):
    if not isinstance(h, tuple):
        return [h], [pl.BlockSpec((tm, width), lambda *ids: (ids[0], col(*ids)))], None
    head, tail = h
    assert tail.shape[0] == tm and head.shape[0] % tm == 0
    head_tiles = head.shape[0] // tm
    specs = [pl.BlockSpec((tm, width), lambda *ids: (jnp.minimum(ids[0], head_tiles - 1), col(*ids))),
             pl.BlockSpec((tm, width), lambda *ids: (0, col(*ids)))]
    return [head, tail], specs, head_tiles


def _read_rows(refs, head_tiles):
    if head_tiles is None:
        return refs[0][...]
    return jnp.where(pl.program_id(0) < head_tiles, refs[0][...], refs[1][...])


def _rmsnorm_kernel(*refs, head_tiles):
    *h_refs, g_ref, o_ref = refs
    x = _read_rows(h_refs, head_tiles)
    y = x * lax.rsqrt(jnp.mean(x * x, axis=-1, keepdims=True) + NORM_EPS)
    o_ref[...] = (y * g_ref[...]).astype(o_ref.dtype)


def _rmsnorm(h, gain, out_dtype=BF16, tm=ROW_TILE):
    d = gain.shape[0]
    arrays, specs, head_tiles = _row_source(h, tm, d, lambda *ids: 0)
    m = sum(a.shape[0] for a in arrays)
    return pl.pallas_call(
        functools.partial(_rmsnorm_kernel, head_tiles=head_tiles),
        grid=(m // tm,),
        in_specs=[*specs, pl.BlockSpec((1, d), lambda i: (0, 0))],
        out_specs=pl.BlockSpec((tm, d), lambda i: (i, 0)),
        out_shape=jax.ShapeDtypeStruct((m, d), out_dtype),
        compiler_params=_params("arbitrary"),
    )(*arrays, gain.reshape(1, d))


def _mm_kernel(a_ref, w_ref, o_ref):
    o_ref[...] = _dot(a_ref[...], w_ref[...]).astype(o_ref.dtype)


def _mm_res_kernel(a_ref, w_ref, *refs, head_tiles):
    *r_refs, o_ref = refs
    o_ref[...] = _read_rows(r_refs, head_tiles) + _dot(a_ref[...], w_ref[...])


def _mm_swiglu_kernel(a_ref, wg_ref, wu_ref, o_ref, wgb_ref, wub_ref):
    @pl.when(pl.program_id(1) == 0)
    def _():
        wgb_ref[...] = wg_ref[...].astype(BF16)
        wub_ref[...] = wu_ref[...].astype(BF16)

    a = a_ref[...]
    o_ref[...] = (_silu(_dot(a, wgb_ref[...])) * _dot(a, wub_ref[...])).astype(o_ref.dtype)


def _matmul(a, w, *, tn, out_dtype, tm=ROW_TILE):
    m, k = a.shape
    n = w.shape[1]
    return pl.pallas_call(
        _mm_kernel,
        grid=(n // tn, m // tm),
        in_specs=[pl.BlockSpec((tm, k), lambda j, i: (i, 0)), pl.BlockSpec((k, tn), lambda j, i: (0, j))],
        out_specs=pl.BlockSpec((tm, tn), lambda j, i: (i, j)),
        out_shape=jax.ShapeDtypeStruct((m, n), out_dtype),
        compiler_params=_params("arbitrary", "arbitrary"),
    )(a, w)


def _matmul_residual(a, w, res, *, tn, tm=ROW_TILE, m_rows=None):
    m = a.shape[0] if m_rows is None else m_rows
    k = a.shape[1]
    n = w.shape[1]
    res_arrays, res_specs, head_tiles = _row_source(res, tm, tn, lambda i, j: j)
    return pl.pallas_call(
        functools.partial(_mm_res_kernel, head_tiles=head_tiles),
        grid=(m // tm, n // tn),
        in_specs=[pl.BlockSpec((tm, k), lambda i, j: (i, 0)),
                  pl.BlockSpec((k, tn), lambda i, j: (0, j)),
                  *res_specs],
        out_specs=pl.BlockSpec((tm, tn), lambda i, j: (i, j)),
        out_shape=jax.ShapeDtypeStruct((m, n), F32),
        compiler_params=_params("arbitrary", "arbitrary"),
    )(a, w, *res_arrays)


def _matmul_swiglu(a, wg, wu, *, tn, tm=ROW_TILE):
    m, k = a.shape
    n = wg.shape[1]
    return pl.pallas_call(
        _mm_swiglu_kernel,
        grid=(n // tn, m // tm),
        in_specs=[pl.BlockSpec((tm, k), lambda j, i: (i, 0)),
                  pl.BlockSpec((k, tn), lambda j, i: (0, j)),
                  pl.BlockSpec((k, tn), lambda j, i: (0, j))],
        out_specs=pl.BlockSpec((tm, tn), lambda j, i: (i, j)),
        out_shape=jax.ShapeDtypeStruct((m, n), BF16),
        scratch_shapes=[pltpu.VMEM((k, tn), BF16), pltpu.VMEM((k, tn), BF16)],
        compiler_params=_params("arbitrary", "arbitrary"),
    )(a, wg, wu)


def _pool_kernel(z_ref, halo_ref, wg_ref, sc_ref, o_ref, *, tm, n_real, group):
    i = pl.program_id(0)
    rows = i * tm + lax.broadcasted_iota(jnp.int32, (tm, 1), 0)
    in_meta = rows >= n_real
    pos = (rows % META_BLOCK) - (META_BLOCK - N_META)
    is_pad = in_meta & (pos < 0)
    for gi, w in enumerate(POOL_WINDOWS):
        cols = slice(gi * group, (gi + 1) * group)
        x = z_ref[:, cols]
        s = jnp.concatenate([halo_ref[:, cols], x], axis=0)
        step = 1
        while step < w:
            s = s + pltpu.roll(s, step, axis=0)
            step *= 2
        count = jnp.where(in_meta, jnp.clip(pos + 1, 1, w), w).astype(F32)
        d = s[POOL_HALO:] / count - x
        y = _dot(d.astype(BF16), wg_ref[gi]) * sc_ref[:, cols]
        o_ref[:, cols] = jnp.where(is_pad, 0.0, y).astype(o_ref.dtype)
    rest = o_ref.shape[1] - len(POOL_WINDOWS) * group
    o_ref[:, len(POOL_WINDOWS) * group:] = jnp.zeros((tm, rest), o_ref.dtype)


def _pool_mixer(z, w_group, scale, *, out_width, n_real, seq, tm=ROW_TILE):
    m = z.shape[0]
    n_g, group, _ = w_group.shape
    width = n_g * group
    tiles_per_seq = seq // tm
    hb = tm // POOL_HALO

    def halo_index(i):
        b = i // tiles_per_seq
        meta_tail = (n_real + b * META_BLOCK + META_BLOCK - POOL_HALO) // POOL_HALO
        real = jnp.where(i % tiles_per_seq == 0, meta_tail, i * hb - 1)
        return (jnp.where(i * tm >= n_real, n_real // POOL_HALO, real), 0)

    return pl.pallas_call(
        functools.partial(_pool_kernel, tm=tm, n_real=n_real, group=group),
        grid=(m // tm,),
        in_specs=[pl.BlockSpec((tm, width), lambda i: (i, 0)),
                  pl.BlockSpec((POOL_HALO, width), halo_index),
                  pl.BlockSpec((n_g, group, group), lambda i: (0, 0, 0)),
                  pl.BlockSpec((1, width), lambda i: (0, 0))],
        out_specs=pl.BlockSpec((tm, out_width), lambda i: (i, 0)),
        out_shape=jax.ShapeDtypeStruct((m, out_width), BF16),
        compiler_params=_params("arbitrary"),
    )(z, z, w_group, scale.reshape(1, width))


def _hgrn_tables(c):
    levels = int(math.log2(c))
    t = np.arange(c)
    blocks = [(t[None, :] <= t[:, None])]
    for l in range(1, HGRN_TABLE_LEVELS + 1):
        bs, hs = 1 << l, 1 << (l - 1)
        mid = (t // bs) * bs + hs - 1
        upper = (t % bs) >= hs
        eq = upper[:, None] & (t[None, :] > mid[:, None]) & (t[None, :] <= t[:, None])
        ek = (~upper)[:, None] & (t[None, :] > t[:, None]) & (t[None, :] <= mid[:, None])
        blocks.append(eq | ek)
    sums = np.concatenate(blocks, axis=0).astype(np.float32)
    sums = np.concatenate([sums, sums], axis=1)
    x = t[:, None] ^ t[None, :]
    lev = np.where(t[None, :] > t[:, None], -1,
                   np.floor(np.log2(np.maximum(x, 1))).astype(np.int32) + (x > 0))
    return sums, lev.astype(np.int32), levels


def _hgrn_kernel(q_ref, f_ref, i_ref, g_ref, lbl_ref, gain_ref, sums_ref, lev_ref, y_in_ref, o_ref, state_ref,
                 *, c, levels, heads, layer):
    del y_in_ref
    @pl.when(pl.program_id(2) == 0)
    def _():
        state_ref[...] = jnp.zeros_like(state_ref)

    lev = lev_ref[...]
    hd = HGRN_HEAD_DIM
    head_cols = [slice(h * hd, (h + 1) * hd) for h in range(heads)]
    lbl = lbl_ref[...]
    ex = jnp.exp(lbl - lbl.max(0, keepdims=True))
    lb = ex[0:layer + 1].sum(0, keepdims=True) / ex.sum(0, keepdims=True)
    fg = lb + (1.0 - lb) * jax.nn.sigmoid(f_ref[...])
    log2_f = jnp.log2(fg)
    k = 1.0 - fg
    q = _silu(q_ref[...])
    v = i_ref[...].astype(BF16)
    hi = log2_f.astype(BF16)
    r1 = log2_f - hi.astype(F32)
    mid = r1.astype(BF16)
    lo = (r1 - mid.astype(F32)).astype(BF16)
    e = (_dot(sums_ref[...], jnp.concatenate([hi, mid], axis=0))
         + _dot(sums_ref[:, 0:c], lo))
    cum = e[0:c]
    total = cum[c - 1:c, :]
    q_dec = (q * jnp.exp2(cum)).astype(BF16)
    k_dec = (k * jnp.exp2(total - cum)).astype(BF16)
    carry = jnp.exp2(total)
    states = [state_ref[h] for h in range(heads)]
    o = [_dot_nt(q_dec[:, cs], st.astype(BF16)) for cs, st in zip(head_cols, states)]
    for h, (cs, st) in enumerate(zip(head_cols, states)):
        state_ref[h] = st * carry[:, cs] + _dot_tn(v[:, cs], k_dec[:, cs])
    q_l, k_l = q.astype(BF16), k.astype(BF16)
    scores = [jnp.where(lev == 0, _dot_nt(q_l[:, cs], k_l[:, cs]), 0.0) for cs in head_cols]
    def level_sums(l):
        if l <= HGRN_TABLE_LEVELS:
            return e[l * c:(l + 1) * c]
        bs, hs = 1 << l, 1 << (l - 1)
        parts = []
        for start in range(0, c, bs):
            middle = cum[start + hs - 1:start + hs, :]
            parts += [middle - cum[start:start + hs], cum[start + hs:start + bs] - middle]
        return jnp.concatenate(parts, axis=0)

    for l in range(1, levels + 1):
        dec = jnp.exp2(level_sums(l))
        q_l, k_l = (q * dec).astype(BF16), (k * dec).astype(BF16)
        scores = [jnp.where(lev == l, _dot_nt(q_l[:, cs], k_l[:, cs]), s) for s, cs in zip(scores, head_cols)]
    o = [oh + _dot(s.astype(BF16), v[:, cs]) for oh, s, cs in zip(o, scores, head_cols)]
    o = [oh * lax.rsqrt(jnp.mean(oh * oh, axis=-1, keepdims=True) + NORM_EPS) for oh in o]
    o_ref[...] = (jnp.concatenate(o, axis=1) * gain_ref[...] * _silu(g_ref[...])).astype(o_ref.dtype)


def _hgrn_mixer(z, col0, lb_logits, layer, norm_gain, y, y_col0, *, batch, n_real, seq):
    m = z.shape[0]
    n_lb, width = lb_logits.shape
    c = HGRN_CHUNK
    assert META_BLOCK == c
    hps = HGRN_HEADS_PER_STEP
    wb = hps * HGRN_HEAD_DIM
    n_hg = width // wb
    chunks = seq // c + 1
    sums, lev, levels = _hgrn_tables(c)

    def row_block(b, ci):
        return jnp.where(ci == 0, n_real // c + b, b * (seq // c) + ci - 1)

    def zspec(r):
        cb = (col0 + r * width) // wb
        return pl.BlockSpec((c, wb), lambda b, hg, ci: (row_block(b, ci), cb + hg))

    vec = pl.BlockSpec((1, wb), lambda b, hg, ci: (0, hg))
    return pl.pallas_call(
        functools.partial(_hgrn_kernel, c=c, levels=levels, heads=hps, layer=layer),
        grid=(batch, n_hg, chunks),
        in_specs=[zspec(0), zspec(1), zspec(2), zspec(3),
                  pl.BlockSpec((n_lb, wb), lambda b, hg, ci: (0, hg)), vec,
                  pl.BlockSpec(sums.shape, lambda b, hg, ci: (0, 0)),
                  pl.BlockSpec(lev.shape, lambda b, hg, ci: (0, 0)),
                  pl.BlockSpec(memory_space=pl.ANY)],
        out_specs=pl.BlockSpec((c, wb), lambda b, hg, ci: (row_block(b, ci), y_col0 // wb + hg)),
        out_shape=jax.ShapeDtypeStruct(y.shape, y.dtype),
        scratch_shapes=[pltpu.VMEM((hps, HGRN_HEAD_DIM, HGRN_HEAD_DIM), F32)],
        input_output_aliases={8: 0},
        compiler_params=_params("arbitrary", "arbitrary", "arbitrary"),
    )(z, z, z, z, lb_logits.astype(F32), norm_gain.reshape(1, width),
      jnp.asarray(sums, BF16), jnp.asarray(lev), y)


def _attn_kernel(slope_ref, sink_ref, q_ref, kvc_ref, kvp_ref, kvm_ref, o_ref, *, group, kvs):
    i = pl.program_id(1)
    j = pl.program_id(2)
    hd = ATT_HEAD_DIM
    blk = ATT_BLOCK
    pairs = group // 2
    blocks = kvs * pairs
    low = lax.broadcasted_iota(jnp.int32, (1, 2 * hd), 1) < hd

    def block_diag(kv):
        vk = jnp.concatenate([kv[:, hd:], kv[:, :hd]], axis=1)
        zero = jnp.zeros_like(kv)
        keys = jnp.concatenate([jnp.where(low, kv, zero), jnp.where(low, zero, vk)], axis=0)
        vals = jnp.concatenate([jnp.where(low, vk, zero), jnp.where(low, zero, kv)], axis=0)
        return keys, vals

    def with_ones(vals, n_first):
        shape = (vals.shape[0], 2 * hd)
        first = lax.broadcasted_iota(jnp.int32, shape, 0) < n_first
        ones = jnp.where(first == (lax.broadcasted_iota(jnp.int32, shape, 1) < hd), 1.0, 0.0)
        return jnp.concatenate([vals, ones.astype(BF16)], axis=1)

    pad = jnp.zeros((2 * hd - 2 * N_META, 2 * hd), BF16)
    s, sm, v_band, v_meta = [], [], [], []
    for a in range(kvs):
        cols = slice(a * 2 * hd, (a + 1) * 2 * hd)
        kb, vb = block_diag(jnp.concatenate([kvp_ref[:, cols], kvc_ref[:, cols]], axis=0))
        km, vm = block_diag(kvm_ref[META_BLOCK - N_META:, cols])
        q2 = jnp.concatenate([q_ref[:, (a * pairs + p) * 2 * hd:(a * pairs + p + 1) * 2 * hd]
                              for p in range(pairs)], axis=0)
        q2 = (q2.astype(F32) * hd ** -0.5).astype(BF16)
        s.append(_dot_nt(q2, kb))
        sm.append(_dot_nt(q2, jnp.concatenate([km, pad], axis=0)))
        v_band.append(with_ones(vb, 2 * blk))
        v_meta.append(with_ones(jnp.concatenate([vm, pad], axis=0), N_META))
    s = jnp.concatenate(s, axis=0)
    sm = jnp.concatenate(sm, axis=0)

    r = lax.broadcasted_iota(jnp.int32, (blk, 2 * blk), 0)
    cidx = lax.broadcasted_iota(jnp.int32, (blk, 2 * blk), 1)
    dist = r - cidx + blk
    band_ok = (dist >= 0) & (dist < blk) & ((cidx >= blk) | (i > 0))
    neg_dist = jnp.where(band_ok, -dist.astype(F32), -jnp.inf)
    neg_dist = jnp.concatenate([neg_dist] * blocks, axis=0)

    def per_row(ref, half):
        return jnp.concatenate([jnp.full((blk, 1), ref[j * kvs * group + 2 * p + half], F32)
                                for p in range(blocks)], axis=0)

    mlane = lax.broadcasted_iota(jnp.int32, (1, 2 * hd), 1)
    halves = []
    for half in range(2):
        sink = per_row(sink_ref, half)
        lb = s[:, half * 2 * blk:(half + 1) * 2 * blk] + per_row(slope_ref, half) * neg_dist
        lm = jnp.where((mlane >= half * N_META) & (mlane < (half + 1) * N_META), sm, -jnp.inf)
        mx = jnp.maximum(jnp.maximum(jnp.maximum(lb[:, :blk], lb[:, blk:]), lm).max(-1, keepdims=True), sink)
        halves.append((jnp.exp(lb - mx), mx, jnp.exp(sink - mx)))
    pb = jnp.concatenate([halves[0][0], halves[1][0]], axis=1).astype(BF16)
    lm = jnp.where(mlane < 2 * N_META, sm, -jnp.inf)
    pm = jnp.exp(lm - jnp.where(mlane < N_META, halves[0][1], halves[1][1])).astype(BF16)
    rows = [slice(a * pairs * blk, (a + 1) * pairs * blk) for a in range(kvs)]
    acc = jnp.concatenate([_dot(pb[rs], vb) + _dot(pm[rs], vm) for rs, vb, vm in zip(rows, v_band, v_meta)],
                          axis=0)
    o = acc[:, :2 * hd] / (acc[:, 2 * hd:] + jnp.where(low, halves[0][2], halves[1][2]))
    for p in range(blocks):
        o_ref[:, p * 2 * hd:(p + 1) * 2 * hd] = o[p * blk:(p + 1) * blk].astype(o_ref.dtype)


def _attention(z, slopes, sinks, *, batch, n_real, seq):
    hd = ATT_HEAD_DIM
    n_q = slopes.shape[0]
    group = n_q // ATT_KV_HEADS
    kvs = ATT_KV_PER_STEP
    qw = kvs * group * hd
    kvw = kvs * 2 * hd
    kv0 = n_q * hd // kvw
    nblk = seq // ATT_BLOCK
    smem = pl.BlockSpec(memory_space=pltpu.SMEM)
    kv_spec = lambda rows: pl.BlockSpec((ATT_BLOCK, kvw), lambda b, i, j: (rows(b, i), kv0 + j))
    return pl.pallas_call(
        functools.partial(_attn_kernel, group=group, kvs=kvs),
        grid=(batch, nblk, ATT_KV_HEADS // kvs),
        in_specs=[smem, smem,
                  pl.BlockSpec((ATT_BLOCK, qw), lambda b, i, j: (b * nblk + i, j)),
                  kv_spec(lambda b, i: b * nblk + i),
                  kv_spec(lambda b, i: b * nblk + jnp.maximum(i - 1, 0)),
                  kv_spec(lambda b, i: n_real // ATT_BLOCK + b)],
        out_specs=pl.BlockSpec((ATT_BLOCK, qw), lambda b, i, j: (b * nblk + i, j)),
        out_shape=jax.ShapeDtypeStruct((n_real, n_q * hd), BF16),
        compiler_params=_params("arbitrary", "arbitrary", "arbitrary"),
    )(slopes, sinks, z, z, z, z)


def _route_kernel(h_ref, g_ref, r_ref, u_ref, route_ref, count_ref, carry_ref, *, tm):
    @pl.when(pl.program_id(0) == 0)
    def _():
        carry_ref[...] = jnp.zeros_like(carry_ref)

    x = h_ref[...]
    u = x * lax.rsqrt(jnp.mean(x * x, axis=-1, keepdims=True) + NORM_EPS) * g_ref[...]
    bits = lax.bitcast_convert_type(u.astype(BF16).astype(F32), jnp.uint32)
    half = bits.shape[1] // 2
    u_ref[...] = (bits[:, :half] >> 16) | (bits[:, half:] & jnp.uint32(0xFFFF0000))
    u_hi = u.astype(BF16)
    rem = u - u_hi.astype(F32)
    u_mid = rem.astype(BF16)
    u_lo = (rem - u_mid.astype(F32)).astype(BF16)
    router = r_ref[...]
    parts = _dot(u_hi, router) + _dot(u_mid, router) + _dot(u_lo, router)
    logits = (parts + pltpu.roll(parts, LANES - N_EXPERTS, axis=1)
              + pltpu.roll(parts, LANES - 2 * N_EXPERTS, axis=1))
    lane = lax.broadcasted_iota(jnp.int32, (tm, LANES), 1)
    lg = jnp.where(lane < N_EXPERTS, logits, -jnp.inf)
    m1 = lg.max(-1, keepdims=True)
    i1 = jnp.where(lg == m1, lane, LANES).min(-1, keepdims=True)
    lg2 = jnp.where(lane == i1, -jnp.inf, lg)
    m2 = lg2.max(-1, keepdims=True)
    i2 = jnp.where(lg2 == m2, lane, LANES).min(-1, keepdims=True)
    e2 = jnp.exp(m2 - m1)
    g1 = 1.0 / (1.0 + e2)
    g2 = e2 / (1.0 + e2)
    chosen = (lane == i1) | (lane == i2)
    onehot = jnp.where(chosen, 1.0, 0.0)
    rr = lax.broadcasted_iota(jnp.int32, (tm, tm), 0)
    cc = lax.broadcasted_iota(jnp.int32, (tm, tm), 1)
    before = jnp.where(cc < rr, 1.0, 0.0).astype(BF16)
    rank = _dot(before, onehot.astype(BF16)) + carry_ref[...]
    r1 = jnp.where(lane == i1, rank, 0.0).sum(-1, keepdims=True)
    r2 = jnp.where(lane == i2, rank, 0.0).sum(-1, keepdims=True)
    carry_ref[...] = carry_ref[...] + onehot.sum(0, keepdims=True)
    count_ref[...] = carry_ref[...]
    packed = jnp.where(lane == 0, i1.astype(F32), 0.0)
    for idx, val in ((1, i2.astype(F32)), (2, g1), (3, g2), (4, r1), (5, r2)):
        packed = jnp.where(lane == idx, val, packed)
    route_ref[...] = packed


def _route(h, gain, router, *, n_rows, tm=ROW_TILE):
    d = h.shape[1]
    r_hi = router.astype(BF16)
    rem = router.astype(F32) - r_hi.astype(F32)
    r_mid = rem.astype(BF16)
    r_lo = (rem - r_mid.astype(F32)).astype(BF16)
    router_pad = jnp.pad(jnp.concatenate([r_hi, r_mid, r_lo], axis=1), ((0, 0), (0, LANES - 3 * N_EXPERTS)))
    return pl.pallas_call(
        functools.partial(_route_kernel, tm=tm),
        grid=(n_rows // tm,),
        in_specs=[pl.BlockSpec((tm, d), lambda i: (i, 0)),
                  pl.BlockSpec((1, d), lambda i: (0, 0)),
                  pl.BlockSpec((d, LANES), lambda i: (0, 0))],
        out_specs=[pl.BlockSpec((tm, d // 2), lambda i: (i, 0)),
                   pl.BlockSpec((tm, LANES), lambda i: (i, 0)),
                   pl.BlockSpec((1, LANES), lambda i: (0, 0))],
        out_shape=[jax.ShapeDtypeStruct((n_rows, d // 2), jnp.uint32),
                   jax.ShapeDtypeStruct((n_rows, LANES), F32),
                   jax.ShapeDtypeStruct((1, LANES), F32)],
        scratch_shapes=[pltpu.VMEM((1, LANES), F32)],
        compiler_params=_params("arbitrary"),
    )(h, gain.reshape(1, d), router_pad)


def _dispatch_kernel(slot_ref, u_ref, xs_in_ref, xs_ref, sem, *, tt):
    del xs_in_ref

    def copy(t, kk):
        return pltpu.make_async_copy(u_ref.at[pl.ds(t, 1), :],
                                     xs_ref.at[pl.ds(slot_ref[0, 0, TOP_K * t + kk], 1), :], sem)

    def issue(g, carry):
        base = pl.multiple_of(g * ROWS_PER_TRIP, ROWS_PER_TRIP)
        for r in range(ROWS_PER_TRIP):
            for kk in range(TOP_K):
                copy(base + r, kk).start()
        return carry

    lax.fori_loop(0, tt // ROWS_PER_TRIP, issue, 0)
    for _ in range(TOP_K):
        pltpu.make_async_copy(u_ref, xs_ref.at[pl.ds(0, tt), :], sem).wait()


def _dispatch(u, slots, n_slots, *, tt=ROW_TILE):
    n, d = u.shape
    return pl.pallas_call(
        functools.partial(_dispatch_kernel, tt=tt),
        grid=(n // tt,),
        in_specs=[pl.BlockSpec((1, 1, TOP_K * tt), lambda i: (i, 0, 0), memory_space=pltpu.SMEM),
                  pl.BlockSpec((tt, d), lambda i: (i, 0)),
                  pl.BlockSpec(memory_space=pl.ANY)],
        out_specs=pl.BlockSpec(memory_space=pl.ANY),
        out_shape=jax.ShapeDtypeStruct((n_slots, d), u.dtype),
        scratch_shapes=[pltpu.SemaphoreType.DMA(())],
        input_output_aliases={2: 0},
        compiler_params=_params("arbitrary"),
    )(slots.reshape(n // tt, 1, TOP_K * tt), u, jnp.zeros((n_slots, d), u.dtype))


def _new_expert(be_ref, i):
    return (i == 0) | (be_ref[i] != be_ref[jnp.maximum(i - 1, 0)])


def _moe_up_kernel(be_ref, nb_ref, a_ref, wg_ref, wu_ref, o_ref, wgb_ref, wub_ref):
    i = pl.program_id(1)

    @pl.when(_new_expert(be_ref, i))
    def _():
        wgb_ref[...] = wg_ref[0].astype(BF16)
        wub_ref[...] = wu_ref[0].astype(BF16)

    @pl.when(i < nb_ref[0])
    def _():
        packed = a_ref[...]
        half = packed.shape[1]
        a_lo = lax.bitcast_convert_type(packed << 16, F32).astype(BF16)
        a_hi = lax.bitcast_convert_type(packed & jnp.uint32(0xFFFF0000), F32).astype(BF16)
        gate = _dot(a_lo, wgb_ref[:half, :]) + _dot(a_hi, wgb_ref[half:, :])
        up = _dot(a_lo, wub_ref[:half, :]) + _dot(a_hi, wub_ref[half:, :])
        o_ref[...] = (_silu(gate) * up).astype(o_ref.dtype)

    @pl.when(i >= nb_ref[0])
    def _():
        o_ref[...] = jnp.zeros_like(o_ref)


def _moe_up(xs, block_expert, n_used, wg, wu, *, tn, tm=ROW_TILE):
    ns, kp = xs.shape
    k, n = wg.shape[1:]
    assert k == 2 * kp
    rows = lambda j, i, be, nb: (jnp.minimum(i, nb[0] - 1), 0)
    wspec = pl.BlockSpec((1, k, tn), lambda j, i, be, nb: (be[i], 0, j))
    return pl.pallas_call(
        _moe_up_kernel,
        grid_spec=pltpu.PrefetchScalarGridSpec(
            num_scalar_prefetch=2,
            grid=(n // tn, ns // tm),
            in_specs=[pl.BlockSpec((tm, kp), rows), wspec, wspec],
            out_specs=pl.BlockSpec((tm, tn), lambda j, i, be, nb: (i, j)),
            scratch_shapes=[pltpu.VMEM((k, tn), BF16), pltpu.VMEM((k, tn), BF16)]),
        out_shape=jax.ShapeDtypeStruct((ns, n), BF16),
        compiler_params=_params("arbitrary", "arbitrary"),
    )(block_expert, n_used, xs, wg, wu)


def _moe_down_kernel(be_ref, nb_ref, a_ref, w_ref, o_ref, wb_ref):
    i = pl.program_id(1)

    @pl.when(_new_expert(be_ref, i))
    def _():
        wb_ref[...] = w_ref[0].astype(BF16)

    @pl.when(i < nb_ref[0])
    def _():
        o_ref[...] = _dot(a_ref[...], wb_ref[...])

    @pl.when(i >= nb_ref[0])
    def _():
        o_ref[...] = jnp.zeros_like(o_ref)


def _moe_down(hmid, block_expert, n_used, w, *, tn, tm=ROW_TILE):
    ns, k = hmid.shape
    n = w.shape[2]
    return pl.pallas_call(
        _moe_down_kernel,
        grid_spec=pltpu.PrefetchScalarGridSpec(
            num_scalar_prefetch=2,
            grid=(n // tn, ns // tm),
            in_specs=[pl.BlockSpec((tm, k), lambda j, i, be, nb: (jnp.minimum(i, nb[0] - 1), 0)),
                      pl.BlockSpec((1, k, tn), lambda j, i, be, nb: (be[i], 0, j))],
            out_specs=pl.BlockSpec((tm, tn), lambda j, i, be, nb: (i, j)),
            scratch_shapes=[pltpu.VMEM((k, tn), BF16)]),
        out_shape=jax.ShapeDtypeStruct((ns, n), F32),
        compiler_params=_params("arbitrary", "arbitrary"),
    )(block_expert, n_used, hmid, w)


def _combine_kernel(slot_ref, next_slot_ref, h_ref, route_ref, g_ref, y_ref, o_ref, buf_ref, sem, *, tt):
    i = pl.program_id(0)
    cur = i % 2

    def copy(slots, buf, t, kk):
        return pltpu.make_async_copy(y_ref.at[pl.ds(slots[0, 0, TOP_K * t + kk], 1), :],
                                     buf_ref.at[buf, kk, pl.ds(t, 1), :], sem.at[buf])

    def fetch(slots, buf):
        def body(g, carry):
            base = pl.multiple_of(g * ROWS_PER_TRIP, ROWS_PER_TRIP)
            for r in range(ROWS_PER_TRIP):
                for kk in range(TOP_K):
                    copy(slots, buf, base + r, kk).start()
            return carry
        lax.fori_loop(0, tt // ROWS_PER_TRIP, body, 0)

    @pl.when(i == 0)
    def _():
        fetch(slot_ref, cur)

    @pl.when(i + 1 < pl.num_programs(0))
    def _():
        fetch(next_slot_ref, 1 - cur)

    for kk in range(TOP_K):
        pltpu.make_async_copy(y_ref.at[pl.ds(0, tt), :], buf_ref.at[cur, kk], sem.at[cur]).wait()
    route = route_ref[...]
    x = h_ref[...] + (buf_ref[cur, 0] * route[:, TOP_K:TOP_K + 1] + buf_ref[cur, 1] * route[:, TOP_K + 1:TOP_K + 2])
    y = x * lax.rsqrt(jnp.mean(x * x, axis=-1, keepdims=True) + NORM_EPS)
    o_ref[...] = y * g_ref[...]


def _combine_norm(h, y, slots, route, gain, *, n_rows, tt=256):
    d = h.shape[1]
    steps = n_rows // tt
    slots = slots.reshape(steps, 1, TOP_K * tt)
    return pl.pallas_call(
        functools.partial(_combine_kernel, tt=tt),
        grid=(steps,),
        in_specs=[pl.BlockSpec((1, 1, TOP_K * tt), lambda i: (i, 0, 0), memory_space=pltpu.SMEM),
                  pl.BlockSpec((1, 1, TOP_K * tt), lambda i: (jnp.minimum(i + 1, steps - 1), 0, 0),
                               memory_space=pltpu.SMEM),
                  pl.BlockSpec((tt, d), lambda i: (i, 0)),
                  pl.BlockSpec((tt, LANES), lambda i: (i, 0)),
                  pl.BlockSpec((1, d), lambda i: (0, 0)),
                  pl.BlockSpec(memory_space=pl.ANY)],
        out_specs=pl.BlockSpec((tt, d), lambda i: (i, 0)),
        out_shape=jax.ShapeDtypeStruct((n_rows, d), F32),
        scratch_shapes=[pltpu.VMEM((2, TOP_K, tt, d), F32), pltpu.SemaphoreType.DMA((2,))],
        compiler_params=_params("arbitrary"),
    )(slots, slots, h, route, gain.reshape(1, d), y)


def _even_layer(h, norm_mix, norm_ffn, w_in, w_out, pool_w, pool_scale, lb_logits, layer, hgrn_norm,
                w_gate, w_up, w_down, *, batch, n_real, seq):
    pool_width = pool_scale.shape[0]
    u = _rmsnorm(h, norm_mix)
    z = _matmul(u, w_in.astype(BF16), tn=512, out_dtype=F32, tm=3 * ROW_TILE)
    y = _pool_mixer(z, pool_w.astype(BF16), pool_scale, out_width=w_out.shape[0], n_real=n_real, seq=seq)
    y = _hgrn_mixer(z, pool_width, lb_logits, layer, hgrn_norm, y, pool_width, batch=batch, n_real=n_real, seq=seq)
    h = _matmul_residual(y, w_out.astype(BF16), h, tn=1024)
    u = _rmsnorm(h, norm_ffn)
    mid = _matmul_swiglu(u, w_gate, w_up, tn=256, tm=3 * ROW_TILE)
    return _matmul_residual(mid, w_down.astype(BF16), h, tn=512)


def _odd_layer(h, norm_mix, norm_ffn, final_norm, w_in, w_out, sinks, router, w_gate, w_up, w_down,
               *, batch, n_real, seq):
    d = h.shape[1]
    hd = ATT_HEAD_DIM
    n_q = sinks.shape[0]
    nq = n_q * hd
    nkv = ATT_KV_HEADS * hd
    u = _rmsnorm(h, norm_mix)
    w = w_in.astype(BF16)
    wk = w[:, nq:nq + nkv].reshape(d, ATT_KV_HEADS, hd)
    wv = w[:, nq + nkv:].reshape(d, ATT_KV_HEADS, hd)
    w_perm = jnp.concatenate([w[:, :nq], jnp.concatenate([wk, wv], axis=2).reshape(d, 2 * nkv)], axis=1)
    z = _matmul(u, w_perm, tn=512, out_dtype=BF16, tm=3 * ROW_TILE)
    slopes = 2.0 ** (-8.0 * jnp.arange(1, n_q + 1, dtype=F32) / n_q)
    att = _attention(z, slopes, sinks.astype(F32), batch=batch, n_real=n_real, seq=seq)
    h = _matmul_residual(att, w_out.astype(BF16), h, tn=1024, m_rows=n_real)
    return _moe_ffn_norm(h, norm_ffn, final_norm, router, w_gate, w_up, w_down, n_real=n_real)


def _moe_ffn_norm(h, norm_ffn, final_norm, router, w_gate, w_up, w_down, *, n_real):
    u, route, counts = _route(h, norm_ffn, router, n_rows=n_real)
    blk = ROW_TILE
    n_blocks = n_real * TOP_K // blk + N_EXPERTS
    e_idx = route[:, 0:TOP_K].astype(jnp.int32)
    rank = route[:, 2 * TOP_K:3 * TOP_K].astype(jnp.int32)
    cnt = counts[0, :N_EXPERTS].astype(jnp.int32)
    padded = (cnt + blk - 1) // blk * blk
    pad_end = jnp.cumsum(padded)
    pad_start = pad_end - padded
    slots = pad_start[e_idx] + rank
    n_used = (pad_end[-1] // blk).astype(jnp.int32).reshape(1)
    blocks = jnp.minimum(jnp.arange(n_blocks, dtype=jnp.int32), n_used[0] - 1)
    block_expert = jnp.minimum(jnp.searchsorted(pad_end, blocks * blk, side='right'),
                               N_EXPERTS - 1).astype(jnp.int32)
    xs = _dispatch(u, slots, n_blocks * blk)
    mid = _moe_up(xs, block_expert, n_used, w_gate, w_up, tn=512)
    y = _moe_down(mid, block_expert, n_used, w_down, tn=min(512, h.shape[1]))
    return _combine_norm(h, y, slots, route, final_norm, n_rows=n_real)


def kernel(x, meta_tokens, norm_mix, norm_ffn, final_norm, even_w_in, even_w_out, pool_w_group, pool_scale, hgrn_lb_logits, hgrn_norm, odd_w_in, odd_w_out, attn_sinks, ffn_w_gate, ffn_w_up, ffn_w_down, moe_router, moe_w_gate, moe_w_up, moe_w_down):
    batch, seq, d = x.shape
    n_real = batch * seq
    assert seq % ROW_TILE == 0 and (batch * META_BLOCK) % ROW_TILE == 0
    assert norm_mix.shape[0] == 2, "one even and one odd layer"
    meta_block = jnp.concatenate([jnp.zeros((META_BLOCK - N_META, d), F32), meta_tokens.astype(F32)], axis=0)
    h = (x.reshape(n_real, d), jnp.tile(meta_block, (batch, 1)))
    dims = dict(batch=batch, n_real=n_real, seq=seq)
    h = _even_layer(h, norm_mix[0], norm_ffn[0], even_w_in[0], even_w_out[0], pool_w_group[0],
                    pool_scale[0], hgrn_lb_logits, 0, hgrn_norm[0], ffn_w_gate[0], ffn_w_up[0],
                    ffn_w_down[0], **dims)
    out = _odd_layer(h, norm_mix[1], norm_ffn[1], final_norm, odd_w_in[0], odd_w_out[0], attn_sinks[0],
                     moe_router[0], moe_w_gate[0], moe_w_up[0], moe_w_down[0], **dims)
    return out.reshape(batch, seq, d)
```

```python
import functools
import math

import numpy as np
import jax
import jax.numpy as jnp
from jax import lax
from jax.experimental import pallas as pl
from jax.experimental.pallas import tpu as pltpu

F32 = jnp.float32
BF16 = jnp.bfloat16

N_META = 16
NORM_EPS = 1e-5
POOL_WINDOWS = (2, 4, 8, 16)
POOL_HALO = 16
HGRN_HEAD_DIM = 128
HGRN_CHUNK = 128
HGRN_HEADS_PER_STEP = 16
HGRN_TABLE_LEVELS = 3
ATT_HEAD_DIM = 64
ATT_KV_HEADS = 8
ATT_BLOCK = 128
ATT_KV_PER_STEP = 8
N_EXPERTS = 8
TOP_K = 2

META_BLOCK = 128
ROW_TILE = 512
LANES = 128
ROWS_PER_TRIP = 8
VMEM_LIMIT = 56 * 1024 * 1024


def _params(*sem):
    return pltpu.CompilerParams(dimension_semantics=sem, vmem_limit_bytes=VMEM_LIMIT)


def _silu(x):
    return x * jax.nn.sigmoid(x)


def _dot(a, b):
    return jnp.dot(a, b, preferred_element_type=F32)


def _dot_nt(a, b):
    return lax.dot_general(a, b, (((1,), (1,)), ((), ())), preferred_element_type=F32)


def _dot_tn(a, b):
    return lax.dot_general(a, b, (((0,), (0,)), ((), ())), preferred_element_type=F32)


def _row_source(h, tm, width, col):
    if not isinstance(h, tuple):
        return [h], [pl.BlockSpec((tm, width), lambda *ids: (ids[0], col(*ids)))], None
    head, tail = h
    assert tail.shape[0] == tm and head.shape[0] % tm == 0
    head_tiles = head.shape[0] // tm
    specs = [pl.BlockSpec((tm, width), lambda *ids: (jnp.minimum(ids[0], head_tiles - 1), col(*ids))),
             pl.BlockSpec((tm, width), lambda *ids: (0, col(*ids)))]
    return [head, tail], specs, head_tiles


def _read_rows(refs, head_tiles):
    if head_tiles is None:
        return refs[0][...]
    return jnp.where(pl.program_id(0) < head_tiles, refs[0][...], refs[1][...])


def _rmsnorm_kernel(*refs, head_tiles):
    *h_refs, g_ref, o_ref = refs
    x = _read_rows(h_refs, head_tiles)
    y = x * lax.rsqrt(jnp.mean(x * x, axis=-1, keepdims=True) + NORM_EPS)
    o_ref[...] = (y * g_ref[...]).astype(o_ref.dtype)


def _rmsnorm(h, gain, out_dtype=BF16, tm=ROW_TILE):
    d = gain.shape[0]
    arrays, specs, head_tiles = _row_source(h, tm, d, lambda *ids: 0)
    m = sum(a.shape[0] for a in arrays)
    return pl.pallas_call(
        functools.partial(_rmsnorm_kernel, head_tiles=head_tiles),
        grid=(m // tm,),
        in_specs=[*specs, pl.BlockSpec((1, d), lambda i: (0, 0))],
        out_specs=pl.BlockSpec((tm, d), lambda i: (i, 0)),
        out_shape=jax.ShapeDtypeStruct((m, d), out_dtype),
        compiler_params=_params("arbitrary"),
    )(*arrays, gain.reshape(1, d))


def _mm_kernel(a_ref, w_ref, o_ref):
    o_ref[...] = _dot(a_ref[...], w_ref[...]).astype(o_ref.dtype)


def _mm_res_kernel(a_ref, w_ref, *refs, head_tiles):
    *r_refs, o_ref = refs
    o_ref[...] = _read_rows(r_refs, head_tiles) + _dot(a_ref[...], w_ref[...])


def _mm_swiglu_kernel(a_ref, wg_ref, wu_ref, o_ref, wgb_ref, wub_ref):
    @pl.when(pl.program_id(1) == 0)
    def _():
        wgb_ref[...] = wg_ref[...].astype(BF16)
        wub_ref[...] = wu_ref[...].astype(BF16)

    a = a_ref[...]
    o_ref[...] = (_silu(_dot(a, wgb_ref[...])) * _dot(a, wub_ref[...])).astype(o_ref.dtype)


def _matmul(a, w, *, tn, out_dtype, tm=ROW_TILE):
    m, k = a.shape
    n = w.shape[1]
    return pl.pallas_call(
        _mm_kernel,
        grid=(n // tn, m // tm),
        in_specs=[pl.BlockSpec((tm, k), lambda j, i: (i, 0)), pl.BlockSpec((k, tn), lambda j, i: (0, j))],
        out_specs=pl.BlockSpec((tm, tn), lambda j, i: (i, j)),
        out_shape=jax.ShapeDtypeStruct((m, n), out_dtype),
        compiler_params=_params("arbitrary", "arbitrary"),
    )(a, w)


def _matmul_residual(a, w, res, *, tn, tm=ROW_TILE, m_rows=None):
    m = a.shape[0] if m_rows is None else m_rows
    k = a.shape[1]
    n = w.shape[1]
    res_arrays, res_specs, head_tiles = _row_source(res, tm, tn, lambda i, j: j)
    return pl.pallas_call(
        functools.partial(_mm_res_kernel, head_tiles=head_tiles),
        grid=(m // tm, n // tn),
        in_specs=[pl.BlockSpec((tm, k), lambda i, j: (i, 0)),
                  pl.BlockSpec((k, tn), lambda i, j: (0, j)),
                  *res_specs],
        out_specs=pl.BlockSpec((tm, tn), lambda i, j: (i, j)),
        out_shape=jax.ShapeDtypeStruct((m, n), F32),
        compiler_params=_params("arbitrary", "arbitrary"),
    )(a, w, *res_arrays)


def _matmul_swiglu(a, wg, wu, *, tn, tm=ROW_TILE):
    m, k = a.shape
    n = wg.shape[1]
    return pl.pallas_call(
        _mm_swiglu_kernel,
        grid=(n // tn, m // tm),
        in_specs=[pl.BlockSpec((tm, k), lambda j, i: (i, 0)),
                  pl.BlockSpec((k, tn), lambda j, i: (0, j)),
                  pl.BlockSpec((k, tn), lambda j, i: (0, j))],
        out_specs=pl.BlockSpec((tm, tn), lambda j, i: (i, j)),
        out_shape=jax.ShapeDtypeStruct((m, n), BF16),
        scratch_shapes=[pltpu.VMEM((k, tn), BF16), pltpu.VMEM((k, tn), BF16)],
        compiler_params=_params("arbitrary", "arbitrary"),
    )(a, wg, wu)


def _pool_kernel(z_ref, halo_ref, wg_ref, sc_ref, o_ref, *, tm, n_real, group):
    i = pl.program_id(0)
    rows = i * tm + lax.broadcasted_iota(jnp.int32, (tm, 1), 0)
    in_meta = rows >= n_real
    pos = (rows % META_BLOCK) - (META_BLOCK - N_META)
    is_pad = in_meta & (pos < 0)
    for gi, w in enumerate(POOL_WINDOWS):
        cols = slice(gi * group, (gi + 1) * group)
        x = z_ref[:, cols]
        s = jnp.concatenate([halo_ref[:, cols], x], axis=0)
        step = 1
        while step < w:
            s = s + pltpu.roll(s, step, axis=0)
            step *= 2
        count = jnp.where(in_meta, jnp.clip(pos + 1, 1, w), w).astype(F32)
        d = s[POOL_HALO:] / count - x
        y = _dot(d.astype(BF16), wg_ref[gi]) * sc_ref[:, cols]
        o_ref[:, cols] = jnp.where(is_pad, 0.0, y).astype(o_ref.dtype)
    rest = o_ref.shape[1] - len(POOL_WINDOWS) * group
    o_ref[:, len(POOL_WINDOWS) * group:] = jnp.zeros((tm, rest), o_ref.dtype)


def _pool_mixer(z, w_group, scale, *, out_width, n_real, seq, tm=ROW_TILE):
    m = z.shape[0]
    n_g, group, _ = w_group.shape
    width = n_g * group
    tiles_per_seq = seq // tm
    hb = tm // POOL_HALO

    def halo_index(i):
        b = i // tiles_per_seq
        meta_tail = (n_real + b * META_BLOCK + META_BLOCK - POOL_HALO) // POOL_HALO
        real = jnp.where(i % tiles_per_seq == 0, meta_tail, i * hb - 1)
        return (jnp.where(i * tm >= n_real, n_real // POOL_HALO, real), 0)

    return pl.pallas_call(
        functools.partial(_pool_kernel, tm=tm, n_real=n_real, group=group),
        grid=(m // tm,),
        in_specs=[pl.BlockSpec((tm, width), lambda i: (i, 0)),
                  pl.BlockSpec((POOL_HALO, width), halo_index),
                  pl.BlockSpec((n_g, group, group), lambda i: (0, 0, 0)),
                  pl.BlockSpec((1, width), lambda i: (0, 0))],
        out_specs=pl.BlockSpec((tm, out_width), lambda i: (i, 0)),
        out_shape=jax.ShapeDtypeStruct((m, out_width), BF16),
        compiler_params=_params("arbitrary"),
    )(z, z, w_group, scale.reshape(1, width))


def _hgrn_tables(c):
    levels = int(math.log2(c))
    t = np.arange(c)
    blocks = [(t[None, :] <= t[:, None])]
    for l in range(1, HGRN_TABLE_LEVELS + 1):
        bs, hs = 1 << l, 1 << (l - 1)
        mid = (t // bs) * bs + hs - 1
        upper = (t % bs) >= hs
        eq = upper[:, None] & (t[None, :] > mid[:, None]) & (t[None, :] <= t[:, None])
        ek = (~upper)[:, None] & (t[None, :] > t[:, None]) & (t[None, :] <= mid[:, None])
        blocks.append(eq | ek)
    sums = np.concatenate(blocks, axis=0).astype(np.float32)
    sums = np.concatenate([sums, sums], axis=1)
    x = t[:, None] ^ t[None, :]
    lev = np.where(t[None, :] > t[:, None], -1,
                   np.floor(np.log2(np.maximum(x, 1))).astype(np.int32) + (x > 0))
    return sums, lev.astype(np.int32), levels


def _hgrn_kernel(q_ref, f_ref, i_ref, g_ref, lbl_ref, gain_ref, sums_ref, lev_ref, y_in_ref, o_ref, state_ref,
                 *, c, levels, heads, layer):
    del y_in_ref
    @pl.when(pl.program_id(2) == 0)
    def _():
        state_ref[...] = jnp.zeros_like(state_ref)

    lev = lev_ref[...]
    hd = HGRN_HEAD_DIM
    head_cols = [slice(h * hd, (h + 1) * hd) for h in range(heads)]
    lbl = lbl_ref[...]
    ex = jnp.exp(lbl - lbl.max(0, keepdims=True))
    lb = ex[0:layer + 1].sum(0, keepdims=True) / ex.sum(0, keepdims=True)
    fg = lb + (1.0 - lb) * jax.nn.sigmoid(f_ref[...])
    log2_f = jnp.log2(fg)
    k = 1.0 - fg
    q = _silu(q_ref[...])
    v = i_ref[...].astype(BF16)
    hi = log2_f.astype(BF16)
    r1 = log2_f - hi.astype(F32)
    mid = r1.astype(BF16)
    lo = (r1 - mid.astype(F32)).astype(BF16)
    e = (_dot(sums_ref[...], jnp.concatenate([hi, mid], axis=0))
         + _dot(sums_ref[:, 0:c], lo))
    cum = e[0:c]
    total = cum[c - 1:c, :]
    q_dec = (q * jnp.exp2(cum)).astype(BF16)
    k_dec = (k * jnp.exp2(total - cum)).astype(BF16)
    carry = jnp.exp2(total)
    states = [state_ref[h] for h in range(heads)]
    o = [_dot_nt(q_dec[:, cs], st.astype(BF16)) for cs, st in zip(head_cols, states)]
    for h, (cs, st) in enumerate(zip(head_cols, states)):
        state_ref[h] = st * carry[:, cs] + _dot_tn(v[:, cs], k_dec[:, cs])
    q_l, k_l = q.astype(BF16), k.astype(BF16)
    scores = [jnp.where(lev == 0, _dot_nt(q_l[:, cs], k_l[:, cs]), 0.0) for cs in head_cols]
    def level_sums(l):
        if l <= HGRN_TABLE_LEVELS:
            return e[l * c:(l + 1) * c]
        bs, hs = 1 << l, 1 << (l - 1)
        parts = []
        for start in range(0, c, bs):
            middle = cum[start + hs - 1:start + hs, :]
            parts += [middle - cum[start:start + hs], cum[start + hs:start + bs] - middle]
        return jnp.concatenate(parts, axis=0)

    for l in range(1, levels + 1):
        dec = jnp.exp2(level_sums(l))
        q_l, k_l = (q * dec).astype(BF16), (k * dec).astype(BF16)
        scores = [jnp.where(lev == l, _dot_nt(q_l[:, cs], k_l[:, cs]), s) for s, cs in zip(scores, head_cols)]
    o = [oh + _dot(s.astype(BF16), v[:, cs]) for oh, s, cs in zip(o, scores, head_cols)]
    o = [oh * lax.rsqrt(jnp.mean(oh * oh, axis=-1, keepdims=True) + NORM_EPS) for oh in o]
    o_ref[...] = (jnp.concatenate(o, axis=1) * gain_ref[...] * _silu(g_ref[...])).astype(o_ref.dtype)


def _hgrn_mixer(z, col0, lb_logits, layer, norm_gain, y, y_col0, *, batch, n_real, seq):
    m = z.shape[0]
    n_lb, width = lb_logits.shape
    c = HGRN_CHUNK
    assert META_BLOCK == c
    hps = HGRN_HEADS_PER_STEP
    wb = hps * HGRN_HEAD_DIM
    n_hg = width // wb
    chunks = seq // c + 1
    sums, lev, levels = _hgrn_tables(c)

    def row_block(b, ci):
        return jnp.where(ci == 0, n_real // c + b, b * (seq // c) + ci - 1)

    def zspec(r):
        cb = (col0 + r * width) // wb
        return pl.BlockSpec((c, wb), lambda b, hg, ci: (row_block(b, ci), cb + hg))

    vec = pl.BlockSpec((1, wb), lambda b, hg, ci: (0, hg))
    return pl.pallas_call(
        functools.partial(_hgrn_kernel, c=c, levels=levels, heads=hps, layer=layer),
        grid=(batch, n_hg, chunks),
        in_specs=[zspec(0), zspec(1), zspec(2), zspec(3),
                  pl.BlockSpec((n_lb, wb), lambda b, hg, ci: (0, hg)), vec,
                  pl.BlockSpec(sums.shape, lambda b, hg, ci: (0, 0)),
                  pl.BlockSpec(lev.shape, lambda b, hg, ci: (0, 0)),
                  pl.BlockSpec(memory_space=pl.ANY)],
        out_specs=pl.BlockSpec((c, wb), lambda b, hg, ci: (row_block(b, ci), y_col0 // wb + hg)),
        out_shape=jax.ShapeDtypeStruct(y.shape, y.dtype),
        scratch_shapes=[pltpu.VMEM((hps, HGRN_HEAD_DIM, HGRN_HEAD_DIM), F32)],
        input_output_aliases={8: 0},
        compiler_params=_params("arbitrary", "arbitrary", "arbitrary"),
    )(z, z, z, z, lb_logits.astype(F32), norm_gain.reshape(1, width),
      jnp.asarray(sums, BF16), jnp.asarray(lev), y)


def _attn_kernel(slope_ref, sink_ref, q_ref, kvc_ref, kvp_ref, kvm_ref, o_ref, *, group, kvs):
    i = pl.program_id(1)
    j = pl.program_id(2)
    hd = ATT_HEAD_DIM
    blk = ATT_BLOCK
    pairs = group // 2
    blocks = kvs * pairs
    low = lax.broadcasted_iota(jnp.int32, (1, 2 * hd), 1) < hd

    def block_diag(kv):
        vk = jnp.concatenate([kv[:, hd:], kv[:, :hd]], axis=1)
        zero = jnp.zeros_like(kv)
        keys = jnp.concatenate([jnp.where(low, kv, zero), jnp.where(low, zero, vk)], axis=0)
        vals = jnp.concatenate([jnp.where(low, vk, zero), jnp.where(low, zero, kv)], axis=0)
        return keys, vals

    def with_ones(vals, n_first):
        shape = (vals.shape[0], 2 * hd)
        first = lax.broadcasted_iota(jnp.int32, shape, 0) < n_first
        ones = jnp.where(first == (lax.broadcasted_iota(jnp.int32, shape, 1) < hd), 1.0, 0.0)
        return jnp.concatenate([vals, ones.astype(BF16)], axis=1)

    pad = jnp.zeros((2 * hd - 2 * N_META, 2 * hd), BF16)
    s, sm, v_band, v_meta = [], [], [], []
    for a in range(kvs):
        cols = slice(a * 2 * hd, (a + 1) * 2 * hd)
        kb, vb = block_diag(jnp.concatenate([kvp_ref[:, cols], kvc_ref[:, cols]], axis=0))
        km, vm = block_diag(kvm_ref[META_BLOCK - N_META:, cols])
        q2 = jnp.concatenate([q_ref[:, (a * pairs + p) * 2 * hd:(a * pairs + p + 1) * 2 * hd]
                              for p in range(pairs)], axis=0)
        q2 = (q2.astype(F32) * hd ** -0.5).astype(BF16)
        s.append(_dot_nt(q2, kb))
        sm.append(_dot_nt(q2, jnp.concatenate([km, pad], axis=0)))
        v_band.append(with_ones(vb, 2 * blk))
        v_meta.append(with_ones(jnp.concatenate([vm, pad], axis=0), N_META))
    s = jnp.concatenate(s, axis=0)
    sm = jnp.concatenate(sm, axis=0)

    r = lax.broadcasted_iota(jnp.int32, (blk, 2 * blk), 0)
    cidx = lax.broadcasted_iota(jnp.int32, (blk, 2 * blk), 1)
    dist = r - cidx + blk
    band_ok = (dist >= 0) & (dist < blk) & ((cidx >= blk) | (i > 0))
    neg_dist = jnp.where(band_ok, -dist.astype(F32), -jnp.inf)
    neg_dist = jnp.concatenate([neg_dist] * blocks, axis=0)

    def per_row(ref, half):
        return jnp.concatenate([jnp.full((blk, 1), ref[j * kvs * group + 2 * p + half], F32)
                                for p in range(blocks)], axis=0)

    mlane = lax.broadcasted_iota(jnp.int32, (1, 2 * hd), 1)
    halves = []
    for half in range(2):
        sink = per_row(sink_ref, half)
        lb = s[:, half * 2 * blk:(half + 1) * 2 * blk] + per_row(slope_ref, half) * neg_dist
        lm = jnp.where((mlane >= half * N_META) & (mlane < (half + 1) * N_META), sm, -jnp.inf)
        mx = jnp.maximum(jnp.maximum(jnp.maximum(lb[:, :blk], lb[:, blk:]), lm).max(-1, keepdims=True), sink)
        halves.append((jnp.exp(lb - mx), mx, jnp.exp(sink - mx)))
    pb = jnp.concatenate([halves[0][0], halves[1][0]], axis=1).astype(BF16)
    lm = jnp.where(mlane < 2 * N_META, sm, -jnp.inf)
    pm = jnp.exp(lm - jnp.where(mlane < N_META, halves[0][1], halves[1][1])).astype(BF16)
    rows = [slice(a * pairs * blk, (a + 1) * pairs * blk) for a in range(kvs)]
    acc = jnp.concatenate([_dot(pb[rs], vb) + _dot(pm[rs], vm) for rs, vb, vm in zip(rows, v_band, v_meta)],
                          axis=0)
    o = acc[:, :2 * hd] / (acc[:, 2 * hd:] + jnp.where(low, halves[0][2], halves[1][2]))
    for p in range(blocks):
        o_ref[:, p * 2 * hd:(p + 1) * 2 * hd] = o[p * blk:(p + 1) * blk].astype(o_ref.dtype)


def _attention(z, slopes, sinks, *, batch, n_real, seq):
    hd = ATT_HEAD_DIM
    n_q = slopes.shape[0]
    group = n_q // ATT_KV_HEADS
    kvs = ATT_KV_PER_STEP
    qw = kvs * group * hd
    kvw = kvs * 2 * hd
    kv0 = n_q * hd // kvw
    nblk = seq // ATT_BLOCK
    smem = pl.BlockSpec(memory_space=pltpu.SMEM)
    kv_spec = lambda rows: pl.BlockSpec((ATT_BLOCK, kvw), lambda b, i, j: (rows(b, i), kv0 + j))
    return pl.pallas_call(
        functools.partial(_attn_kernel, group=group, kvs=kvs),
        grid=(batch, nblk, ATT_KV_HEADS // kvs),
        in_specs=[smem, smem,
                  pl.BlockSpec((ATT_BLOCK, qw), lambda b, i, j: (b * nblk + i, j)),
                  kv_spec(lambda b, i: b * nblk + i),
                  kv_spec(lambda b, i: b * nblk + jnp.maximum(i - 1, 0)),
                  kv_spec(lambda b, i: n_real // ATT_BLOCK + b)],
        out_specs=pl.BlockSpec((ATT_BLOCK, qw), lambda b, i, j: (b * nblk + i, j)),
        out_shape=jax.ShapeDtypeStruct((n_real, n_q * hd), BF16),
        compiler_params=_params("arbitrary", "arbitrary", "arbitrary"),
    )(slopes, sinks, z, z, z, z)


def _route_kernel(h_ref, g_ref, r_ref, u_ref, route_ref, count_ref, carry_ref, *, tm):
    @pl.when(pl.program_id(0) == 0)
    def _():
        carry_ref[...] = jnp.zeros_like(carry_ref)

    x = h_ref[...]
    u = x * lax.rsqrt(jnp.mean(x * x, axis=-1, keepdims=True) + NORM_EPS) * g_ref[...]
    bits = lax.bitcast_convert_type(u.astype(BF16).astype(F32), jnp.uint32)
    half = bits.shape[1] // 2
    u_ref[...] = (bits[:, :half] >> 16) | (bits[:, half:] & jnp.uint32(0xFFFF0000))
    u_hi = u.astype(BF16)
    rem = u - u_hi.astype(F32)
    u_mid = rem.astype(BF16)
    u_lo = (rem - u_mid.astype(F32)).astype(BF16)
    router = r_ref[...]
    parts = _dot(u_hi, router) + _dot(u_mid, router) + _dot(u_lo, router)
    logits = (parts + pltpu.roll(parts, LANES - N_EXPERTS, axis=1)
              + pltpu.roll(parts, LANES - 2 * N_EXPERTS, axis=1))
    lane = lax.broadcasted_iota(jnp.int32, (tm, LANES), 1)
    lg = jnp.where(lane < N_EXPERTS, logits, -jnp.inf)
    m1 = lg.max(-1, keepdims=True)
    i1 = jnp.where(lg == m1, lane, LANES).min(-1, keepdims=True)
    lg2 = jnp.where(lane == i1, -jnp.inf, lg)
    m2 = lg2.max(-1, keepdims=True)
    i2 = jnp.where(lg2 == m2, lane, LANES).min(-1, keepdims=True)
    e2 = jnp.exp(m2 - m1)
    g1 = 1.0 / (1.0 + e2)
    g2 = e2 / (1.0 + e2)
    chosen = (lane == i1) | (lane == i2)
    onehot = jnp.where(chosen, 1.0, 0.0)
    rr = lax.broadcasted_iota(jnp.int32, (tm, tm), 0)
    cc = lax.broadcasted_iota(jnp.int32, (tm, tm), 1)
    before = jnp.where(cc < rr, 1.0, 0.0).astype(BF16)
    rank = _dot(before, onehot.astype(BF16)) + carry_ref[...]
    r1 = jnp.where(lane == i1, rank, 0.0).sum(-1, keepdims=True)
    r2 = jnp.where(lane == i2, rank, 0.0).sum(-1, keepdims=True)
    carry_ref[...] = carry_ref[...] + onehot.sum(0, keepdims=True)
    count_ref[...] = carry_ref[...]
    packed = jnp.where(lane == 0, i1.astype(F32), 0.0)
    for idx, val in ((1, i2.astype(F32)), (2, g1), (3, g2), (4, r1), (5, r2)):
        packed = jnp.where(lane == idx, val, packed)
    route_ref[...] = packed


def _route(h, gain, router, *, n_rows, tm=ROW_TILE):
    d = h.shape[1]
    r_hi = router.astype(BF16)
    rem = router.astype(F32) - r_hi.astype(F32)
    r_mid = rem.astype(BF16)
    r_lo = (rem - r_mid.astype(F32)).astype(BF16)
    router_pad = jnp.pad(jnp.concatenate([r_hi, r_mid, r_lo], axis=1), ((0, 0), (0, LANES - 3 * N_EXPERTS)))
    return pl.pallas_call(
        functools.partial(_route_kernel, tm=tm),
        grid=(n_rows // tm,),
        in_specs=[pl.BlockSpec((tm, d), lambda i: (i, 0)),
                  pl.BlockSpec((1, d), lambda i: (0, 0)),
                  pl.BlockSpec((d, LANES), lambda i: (0, 0))],
        out_specs=[pl.BlockSpec((tm, d // 2), lambda i: (i, 0)),
                   pl.BlockSpec((tm, LANES), lambda i: (i, 0)),
                   pl.BlockSpec((1, LANES), lambda i: (0, 0))],
        out_shape=[jax.ShapeDtypeStruct((n_rows, d // 2), jnp.uint32),
                   jax.ShapeDtypeStruct((n_rows, LANES), F32),
                   jax.ShapeDtypeStruct((1, LANES), F32)],
        scratch_shapes=[pltpu.VMEM((1, LANES), F32)],
        compiler_params=_params("arbitrary"),
    )(h, gain.reshape(1, d), router_pad)


def _dispatch_kernel(slot_ref, u_ref, xs_in_ref, xs_ref, sem, *, tt):
    del xs_in_ref

    def copy(t, kk):
        return pltpu.make_async_copy(u_ref.at[pl.ds(t, 1), :],
                                     xs_ref.at[pl.ds(slot_ref[0, 0, TOP_K * t + kk], 1), :], sem)

    def issue(g, carry):
        base = pl.multiple_of(g * ROWS_PER_TRIP, ROWS_PER_TRIP)
        for r in range(ROWS_PER_TRIP):
            for kk in range(TOP_K):
                copy(base + r, kk).start(priority=kk % 2)
        return carry

    lax.fori_loop(0, tt // ROWS_PER_TRIP, issue, 0)
    for _ in range(TOP_K):
        pltpu.make_async_copy(u_ref, xs_ref.at[pl.ds(0, tt), :], sem).wait()


def _dispatch(u, slots, n_slots, *, tt=ROW_TILE):
    n, d = u.shape
    return pl.pallas_call(
        functools.partial(_dispatch_kernel, tt=tt),
        grid=(n // tt,),
        in_specs=[pl.BlockSpec((1, 1, TOP_K * tt), lambda i: (i, 0, 0), memory_space=pltpu.SMEM),
                  pl.BlockSpec((tt, d), lambda i: (i, 0)),
                  pl.BlockSpec(memory_space=pl.ANY)],
        out_specs=pl.BlockSpec(memory_space=pl.ANY),
        out_shape=jax.ShapeDtypeStruct((n_slots, d), u.dtype),
        scratch_shapes=[pltpu.SemaphoreType.DMA(())],
        input_output_aliases={2: 0},
        compiler_params=_params("arbitrary"),
    )(slots.reshape(n // tt, 1, TOP_K * tt), u, jnp.zeros((n_slots, d), u.dtype))


def _new_expert(be_ref, i):
    return (i == 0) | (be_ref[i] != be_ref[jnp.maximum(i - 1, 0)])


def _moe_up_kernel(be_ref, nb_ref, a_ref, wg_ref, wu_ref, o_ref, wgb_ref, wub_ref):
    i = pl.program_id(1)

    @pl.when(_new_expert(be_ref, i))
    def _():
        wgb_ref[...] = wg_ref[0].astype(BF16)
        wub_ref[...] = wu_ref[0].astype(BF16)

    @pl.when(i < nb_ref[0])
    def _():
        packed = a_ref[...]
        half = packed.shape[1]
        a_lo = lax.bitcast_convert_type(packed << 16, F32).astype(BF16)
        a_hi = lax.bitcast_convert_type(packed & jnp.uint32(0xFFFF0000), F32).astype(BF16)
        gate = _dot(a_lo, wgb_ref[:half, :]) + _dot(a_hi, wgb_ref[half:, :])
        up = _dot(a_lo, wub_ref[:half, :]) + _dot(a_hi, wub_ref[half:, :])
        o_ref[...] = (_silu(gate) * up).astype(o_ref.dtype)

    @pl.when(i >= nb_ref[0])
    def _():
        o_ref[...] = jnp.zeros_like(o_ref)


def _moe_up(xs, block_expert, n_used, wg, wu, *, tn, tm=ROW_TILE):
    ns, kp = xs.shape
    k, n = wg.shape[1:]
    assert k == 2 * kp
    rows = lambda j, i, be, nb: (jnp.minimum(i, nb[0] - 1), 0)
    wspec = pl.BlockSpec((1, k, tn), lambda j, i, be, nb: (be[i], 0, j))
    return pl.pallas_call(
        _moe_up_kernel,
        grid_spec=pltpu.PrefetchScalarGridSpec(
            num_scalar_prefetch=2,
            grid=(n // tn, ns // tm),
            in_specs=[pl.BlockSpec((tm, kp), rows), wspec, wspec],
            out_specs=pl.BlockSpec((tm, tn), lambda j, i, be, nb: (i, j)),
            scratch_shapes=[pltpu.VMEM((k, tn), BF16), pltpu.VMEM((k, tn), BF16)]),
        out_shape=jax.ShapeDtypeStruct((ns, n), BF16),
        compiler_params=_params("arbitrary", "arbitrary"),
    )(block_expert, n_used, xs, wg, wu)


def _moe_down_kernel(be_ref, nb_ref, a_ref, w_ref, o_ref, wb_ref):
    i = pl.program_id(1)

    @pl.when(_new_expert(be_ref, i))
    def _():
        wb_ref[...] = w_ref[0].astype(BF16)

    @pl.when(i < nb_ref[0])
    def _():
        o_ref[...] = _dot(a_ref[...], wb_ref[...])

    @pl.when(i >= nb_ref[0])
    def _():
        o_ref[...] = jnp.zeros_like(o_ref)


def _moe_down(hmid, block_expert, n_used, w, *, tn, tm=ROW_TILE):
    ns, k = hmid.shape
    n = w.shape[2]
    return pl.pallas_call(
        _moe_down_kernel,
        grid_spec=pltpu.PrefetchScalarGridSpec(
            num_scalar_prefetch=2,
            grid=(n // tn, ns // tm),
            in_specs=[pl.BlockSpec((tm, k), lambda j, i, be, nb: (jnp.minimum(i, nb[0] - 1), 0)),
                      pl.BlockSpec((1, k, tn), lambda j, i, be, nb: (be[i], 0, j))],
            out_specs=pl.BlockSpec((tm, tn), lambda j, i, be, nb: (i, j)),
            scratch_shapes=[pltpu.VMEM((k, tn), BF16)]),
        out_shape=jax.ShapeDtypeStruct((ns, n), F32),
        compiler_params=_params("arbitrary", "arbitrary"),
    )(block_expert, n_used, hmid, w)


def _combine_kernel(slot_ref, next_slot_ref, h_ref, route_ref, g_ref, y_ref, o_ref, buf_ref, sem, *, tt):
    i = pl.program_id(0)
    cur = i % 2

    def copy(slots, buf, t, kk):
        return pltpu.make_async_copy(y_ref.at[pl.ds(slots[0, 0, TOP_K * t + kk], 1), :],
                                     buf_ref.at[buf, kk, pl.ds(t, 1), :], sem.at[buf])

    def fetch(slots, buf):
        def body(g, carry):
            base = pl.multiple_of(g * ROWS_PER_TRIP, ROWS_PER_TRIP)
            for r in range(ROWS_PER_TRIP):
                for kk in range(TOP_K):
                    copy(slots, buf, base + r, kk).start(priority=kk % 2)
            return carry
        lax.fori_loop(0, tt // ROWS_PER_TRIP, body, 0)

    @pl.when(i == 0)
    def _():
        fetch(slot_ref, cur)

    @pl.when(i + 1 < pl.num_programs(0))
    def _():
        fetch(next_slot_ref, 1 - cur)

    for kk in range(TOP_K):
        pltpu.make_async_copy(y_ref.at[pl.ds(0, tt), :], buf_ref.at[cur, kk], sem.at[cur]).wait()
    route = route_ref[...]
    x = h_ref[...] + (buf_ref[cur, 0] * route[:, TOP_K:TOP_K + 1] + buf_ref[cur, 1] * route[:, TOP_K + 1:TOP_K + 2])
    y = x * lax.rsqrt(jnp.mean(x * x, axis=-1, keepdims=True) + NORM_EPS)
    o_ref[...] = y * g_ref[...]


def _combine_norm(h, y, slots, route, gain, *, n_rows, tt=256):
    d = h.shape[1]
    steps = n_rows // tt
    slots = slots.reshape(steps, 1, TOP_K * tt)
    return pl.pallas_call(
        functools.partial(_combine_kernel, tt=tt),
        grid=(steps,),
        in_specs=[pl.BlockSpec((1, 1, TOP_K * tt), lambda i: (i, 0, 0), memory_space=pltpu.SMEM),
                  pl.BlockSpec((1, 1, TOP_K * tt), lambda i: (jnp.minimum(i + 1, steps - 1), 0, 0),
                               memory_space=pltpu.SMEM),
                  pl.BlockSpec((tt, d), lambda i: (i, 0)),
                  pl.BlockSpec((tt, LANES), lambda i: (i, 0)),
                  pl.BlockSpec((1, d), lambda i: (0, 0)),
                  pl.BlockSpec(memory_space=pl.ANY)],
        out_specs=pl.BlockSpec((tt, d), lambda i: (i, 0)),
        out_shape=jax.ShapeDtypeStruct((n_rows, d), F32),
        scratch_shapes=[pltpu.VMEM((2, TOP_K, tt, d), F32), pltpu.SemaphoreType.DMA((2,))],
        compiler_params=_params("arbitrary"),
    )(slots, slots, h, route, gain.reshape(1, d), y)


def _even_layer(h, norm_mix, norm_ffn, w_in, w_out, pool_w, pool_scale, lb_logits, layer, hgrn_norm,
                w_gate, w_up, w_down, *, batch, n_real, seq):
    pool_width = pool_scale.shape[0]
    u = _rmsnorm(h, norm_mix)
    z = _matmul(u, w_in.astype(BF16), tn=512, out_dtype=F32, tm=3 * ROW_TILE)
    y = _pool_mixer(z, pool_w.astype(BF16), pool_scale, out_width=w_out.shape[0], n_real=n_real, seq=seq)
    y = _hgrn_mixer(z, pool_width, lb_logits, layer, hgrn_norm, y, pool_width, batch=batch, n_real=n_real, seq=seq)
    h = _matmul_residual(y, w_out.astype(BF16), h, tn=1024)
    u = _rmsnorm(h, norm_ffn)
    mid = _matmul_swiglu(u, w_gate, w_up, tn=256, tm=3 * ROW_TILE)
    return _matmul_residual(mid, w_down.astype(BF16), h, tn=512)


def _odd_layer(h, norm_mix, norm_ffn, final_norm, w_in, w_out, sinks, router, w_gate, w_up, w_down,
               *, batch, n_real, seq):
    d = h.shape[1]
    hd = ATT_HEAD_DIM
    n_q = sinks.shape[0]
    nq = n_q * hd
    nkv = ATT_KV_HEADS * hd
    u = _rmsnorm(h, norm_mix)
    w = w_in.astype(BF16)
    wk = w[:, nq:nq + nkv].reshape(d, ATT_KV_HEADS, hd)
    wv = w[:, nq + nkv:].reshape(d, ATT_KV_HEADS, hd)
    w_perm = jnp.concatenate([w[:, :nq], jnp.concatenate([wk, wv], axis=2).reshape(d, 2 * nkv)], axis=1)
    z = _matmul(u, w_perm, tn=512, out_dtype=BF16, tm=3 * ROW_TILE)
    slopes = 2.0 ** (-8.0 * jnp.arange(1, n_q + 1, dtype=F32) / n_q)
    att = _attention(z, slopes, sinks.astype(F32), batch=batch, n_real=n_real, seq=seq)
    h = _matmul_residual(att, w_out.astype(BF16), h, tn=1024, m_rows=n_real)
    return _moe_ffn_norm(h, norm_ffn, final_norm, router, w_gate, w_up, w_down, n_real=n_real)


def _moe_ffn_norm(h, norm_ffn, final_norm, router, w_gate, w_up, w_down, *, n_real):
    u, route, counts = _route(h, norm_ffn, router, n_rows=n_real)
    blk = ROW_TILE
    n_blocks = n_real * TOP_K // blk + N_EXPERTS
    e_idx = route[:, 0:TOP_K].astype(jnp.int32)
    rank = route[:, 2 * TOP_K:3 * TOP_K].astype(jnp.int32)
    cnt = counts[0, :N_EXPERTS].astype(jnp.int32)
    padded = (cnt + blk - 1) // blk * blk
    pad_end = jnp.cumsum(padded)
    pad_start = pad_end - padded
    slots = pad_start[e_idx] + rank
    n_used = (pad_end[-1] // blk).astype(jnp.int32).reshape(1)
    blocks = jnp.minimum(jnp.arange(n_blocks, dtype=jnp.int32), n_used[0] - 1)
    block_expert = jnp.minimum(jnp.searchsorted(pad_end, blocks * blk, side='right'),
                               N_EXPERTS - 1).astype(jnp.int32)
    xs = _dispatch(u, slots, n_blocks * blk)
    mid = _moe_up(xs, block_expert, n_used, w_gate, w_up, tn=512)
    y = _moe_down(mid, block_expert, n_used, w_down, tn=min(512, h.shape[1]))
    return _combine_norm(h, y, slots, route, final_norm, n_rows=n_real)


def kernel(x, meta_tokens, norm_mix, norm_ffn, final_norm, even_w_in, even_w_out, pool_w_group, pool_scale, hgrn_lb_logits, hgrn_norm, odd_w_in, odd_w_out, attn_sinks, ffn_w_gate, ffn_w_up, ffn_w_down, moe_router, moe_w_gate, moe_w_up, moe_w_down):
    batch, seq, d = x.shape
    n_real = batch * seq
    assert seq % ROW_TILE == 0 and (batch * META_BLOCK) % ROW_TILE == 0
    assert norm_mix.shape[0] == 2, "one even and one odd layer"
    meta_block = jnp.concatenate([jnp.zeros((META_BLOCK - N_META, d), F32), meta_tokens.astype(F32)], axis=0)
    h = (x.reshape(n_real, d), jnp.tile(meta_block, (batch, 1)))
    dims = dict(batch=batch, n_real=n_real, seq=seq)
    h = _even_layer(h, norm_mix[0], norm_ffn[0], even_w_in[0], even_w_out[0], pool_w_group[0],
                    pool_scale[0], hgrn_lb_logits, 0, hgrn_norm[0], ffn_w_gate[0], ffn_w_up[0],
                    ffn_w_down[0], **dims)
    out = _odd_layer(h, norm_mix[1], norm_ffn[1], final_norm, odd_w_in[0], odd_w_out[0], attn_sinks[0],
                     moe_router[0], moe_w_gate[0], moe_w_up[0], moe_w_down[0], **dims)
    return out.reshape(batch, seq, d)
```
